```python
import jax, jax.numpy as jnp
from jax import lax
import numpy as np

D_MODEL = 2048
BATCH = 16
SEQ = 2048
DEPTH = 4

W_CONV = D_MODEL // 2
HEAD_DIM = 64
N_Q_HEADS = (D_MODEL // 2) // HEAD_DIM
N_KV_HEADS = N_Q_HEADS // 4
GQA_GROUP = N_Q_HEADS // N_KV_HEADS
W_ATT = N_Q_HEADS * HEAD_DIM
W_KV = N_KV_HEADS * HEAD_DIM
W_MIX = W_CONV + W_ATT
IN_SPLITS = [W_CONV, W_CONV, W_CONV, W_ATT, W_KV, W_KV, W_ATT]
W_IN = sum(IN_SPLITS)
CONV_WIDTH = 31
CONV_PAD = CONV_WIDTH // 2
WINDOW = 128
BLK = 128
NUM_BUCKETS = 32
MAX_DISTANCE = 128
PLE_DIM = 256
EPS = 1e-6
NEG = -1e30

kernel_name = "hybrid_conv_swa_parallel_encoder"


def _rmsnorm(x, g):
    xf = x.astype(jnp.float32)
    y = xf * lax.rsqrt(jnp.mean(xf * xf, axis=-1, keepdims=True) + EPS)
    return (y * g.astype(jnp.float32)).astype(x.dtype)


def _layernorm(x, g, b):
    xf = x.astype(jnp.float32)
    mu = jnp.mean(xf, axis=-1, keepdims=True)
    xc = xf - mu
    y = xc * lax.rsqrt(jnp.mean(xc * xc, axis=-1, keepdims=True) + EPS)
    return (y * g.astype(jnp.float32) + b.astype(jnp.float32)).astype(x.dtype)


def _t5_band_buckets():
    q_off = np.arange(BLK)[:, None]
    k_off = np.arange(3 * BLK)[None, :] - BLK
    rel = k_off - q_off
    half = NUM_BUCKETS // 2
    ret = (rel > 0).astype(np.int32) * half
    n = np.abs(rel)
    max_exact = half // 2
    large = max_exact + (np.log(np.maximum(n, 1) / max_exact)
                         / np.log(MAX_DISTANCE / max_exact)
                         * (half - max_exact)).astype(np.int32)
    large = np.minimum(large, half - 1)
    ret = ret + np.where(n < max_exact, n, large)
    return ret.astype(np.int32), (n <= WINDOW)


def _band_bias(rel_bias):
    buckets, band = _t5_band_buckets()
    bias = jnp.take(rel_bias.astype(jnp.float32), jnp.asarray(buckets), axis=0)
    bias = jnp.transpose(bias, (2, 0, 1))
    bias = jnp.where(jnp.asarray(band)[None], bias, NEG)
    return bias.reshape(N_KV_HEADS, GQA_GROUP, BLK, 3 * BLK)


def _window_attention(q, k, v, bias_band, sink):
    B, S = q.shape[0], q.shape[1]
    nb = S // BLK
    scale = HEAD_DIM ** -0.5
    k_pad = jnp.pad(k, ((0, 0), (BLK, BLK), (0, 0), (0, 0)))
    v_pad = jnp.pad(v, ((0, 0), (BLK, BLK), (0, 0), (0, 0)))
    sink_f = sink.astype(jnp.float32).reshape(N_KV_HEADS, GQA_GROUP, 1)
    key_off = jnp.arange(3 * BLK) - BLK

    def one_block(b):
        start = b * BLK
        qb = lax.dynamic_slice_in_dim(q, start, BLK, axis=1)
        kb = lax.dynamic_slice_in_dim(k_pad, start, 3 * BLK, axis=1)
        vb = lax.dynamic_slice_in_dim(v_pad, start, 3 * BLK, axis=1)
        s = jnp.einsum('bqkgd,bskd->bkgqs', qb, kb,
                       preferred_element_type=jnp.float32) * scale + bias_band
        kpos = start + key_off
        valid = (kpos >= 0) & (kpos < S)
        s = jnp.where(valid, s, NEG)
        m = jnp.maximum(jnp.max(s, axis=-1), sink_f)
        e = jnp.exp(s - m[..., None])
        denom = jnp.sum(e, axis=-1) + jnp.exp(sink_f - m)
        pr = (e / denom[..., None]).astype(vb.dtype)
        return jnp.einsum('bkgqs,bskd->bqkgd', pr, vb)

    out = lax.map(one_block, jnp.arange(nb))
    return jnp.moveaxis(out, 0, 1).reshape(B, S, W_ATT)


def _conformer_conv(a_val, a_glu, conv_w, conv_b, cln_g, cln_b):
    a = a_val * jax.nn.sigmoid(a_glu)
    y = lax.conv_general_dilated(
        a, conv_w[:, None, :].astype(a.dtype), window_strides=(1,),
        padding=[(CONV_PAD, CONV_PAD)], dimension_numbers=('NWC', 'WIO', 'NWC'),
        feature_group_count=W_CONV) + conv_b
    y = _layernorm(y, cln_g, cln_b)
    return jax.nn.silu(y)


def setup_inputs(seed: int = 0) -> dict:
    key = jax.random.key(seed)
    ks = jax.random.split(key, 16)
    f32 = jnp.float32
    nrm = lambda k, shp, s: (jax.random.normal(k, shp, f32) * s)
    return {
        "x": nrm(ks[0], (BATCH, SEQ, D_MODEL), 1.0),
        "p": nrm(ks[1], (DEPTH, BATCH, SEQ, PLE_DIM), 1.0),
        "norm_g": 1.0 + nrm(ks[2], (DEPTH, D_MODEL), 0.02),
        "w_in": nrm(ks[3], (DEPTH, D_MODEL, W_IN), D_MODEL ** -0.5),
        "conv_w": nrm(ks[4], (DEPTH, CONV_WIDTH, W_CONV), CONV_WIDTH ** -0.5),
        "conv_b": nrm(ks[5], (DEPTH, W_CONV), 0.02),
        "cln_g": 1.0 + nrm(ks[6], (DEPTH, W_CONV), 0.02),
        "cln_b": nrm(ks[7], (DEPTH, W_CONV), 0.02),
        "sink": nrm(ks[8], (DEPTH, N_Q_HEADS), 0.5),
        "rel_bias": nrm(ks[9], (NUM_BUCKETS, N_Q_HEADS), 0.5),
        "w_out": nrm(ks[10], (DEPTH, W_MIX, D_MODEL), W_MIX ** -0.5),
        "w_pe": nrm(ks[11], (DEPTH, PLE_DIM, D_MODEL), PLE_DIM ** -0.5),
        "pe_g": 1.0 + nrm(ks[12], (DEPTH, D_MODEL), 0.02),
        "w_pg": nrm(ks[13], (DEPTH, D_MODEL, D_MODEL), D_MODEL ** -0.5),
        "final_g": 1.0 + nrm(ks[14], (D_MODEL,), 0.02),
    }


def reference(x, p, norm_g, w_in, conv_w, conv_b, cln_g, cln_b, sink, rel_bias,
              w_out, w_pe, pe_g, w_pg, final_g):
    B, S, _ = x.shape
    bias_band = _band_bias(rel_bias)
    offs = np.cumsum(IN_SPLITS)[:-1].tolist()
    h = x
    for i in range(DEPTH):
        hn = _rmsnorm(h, norm_g[i])
        u = hn @ w_in[i]
        a_val, a_glu, a_z, q, k, v, b_z = jnp.split(u, offs, axis=-1)
        ya = _conformer_conv(a_val, a_glu, conv_w[i], conv_b[i], cln_g[i], cln_b[i])
        ya = ya * jax.nn.silu(a_z)
        q = q.reshape(B, S, N_KV_HEADS, GQA_GROUP, HEAD_DIM)
        k = k.reshape(B, S, N_KV_HEADS, HEAD_DIM)
        v = v.reshape(B, S, N_KV_HEADS, HEAD_DIM)
        yb = _window_attention(q, k, v, bias_band, sink[i]) * jax.nn.silu(b_z)
        h = h + jnp.concatenate([ya, yb], axis=-1) @ w_out[i]
        e = _rmsnorm(p[i] @ w_pe[i], pe_g[i])
        h = h + e * jax.nn.sigmoid(h @ w_pg[i])
    return _rmsnorm(h, final_g)
```

```python
import functools

import numpy as np
import jax
import jax.numpy as jnp
from jax import lax
from jax.experimental import pallas as pl
from jax.experimental.pallas import tpu as pltpu

D_MODEL = 2048
DEPTH = 4
W_CONV = 1024
HEAD_DIM = 64
N_Q_HEADS = 16
N_KV_HEADS = 4
GQA_GROUP = 4
W_ATT = 1024
W_KV = 256
CONV_WIDTH = 31
CONV_PAD = 15
WINDOW = 128
BLK = 128
NUM_BUCKETS = 32
MAX_DISTANCE = 128
PLE_DIM = 256
EPS = 1e-6
NEG = -1e30

LANES = 128
HALO = 16
N_SLAB = W_CONV // LANES
N_PAIR = GQA_GROUP // 2
W_KV_DUP = N_KV_HEADS * LANES
VMEM_LIMIT = 56 * 1024 * 1024

F32 = jnp.float32
BF16 = jnp.bfloat16


def _sigmoid(x):
    return jax.nn.sigmoid(x)


def _silu(x):
    return x * jax.nn.sigmoid(x)


def _inproj_kernel(h_ref, g_ref, wf_ref, wb_ref, of_ref, ob_ref, hn_ref):
    @pl.when(pl.program_id(1) == 0)
    def _():
        x = h_ref[...]
        ms = jnp.mean(x * x, axis=-1, keepdims=True)
        hn_ref[...] = ((x * lax.rsqrt(ms + EPS)) * g_ref[...]).astype(BF16)

    hn = hn_ref[...]
    of_ref[...] = jnp.dot(hn, wf_ref[...], preferred_element_type=F32)
    ob_ref[...] = jnp.dot(hn, wb_ref[...], preferred_element_type=F32).astype(BF16)


def _inproj(h, g, wf, wb, *, tm=1024, nj=4):
    T = h.shape[0]
    nf, nb = wf.shape[1] // nj, wb.shape[1] // nj
    return pl.pallas_call(
        _inproj_kernel,
        grid=(T // tm, nj),
        in_specs=[
            pl.BlockSpec((tm, D_MODEL), lambda i, j: (i, 0)),
            pl.BlockSpec((1, D_MODEL), lambda i, j: (0, 0)),
            pl.BlockSpec((D_MODEL, nf), lambda i, j: (0, j)),
            pl.BlockSpec((D_MODEL, nb), lambda i, j: (0, j)),
        ],
        out_specs=[
            pl.BlockSpec((tm, nf), lambda i, j: (i, j)),
            pl.BlockSpec((tm, nb), lambda i, j: (i, j)),
        ],
        out_shape=[
            jax.ShapeDtypeStruct((T, wf.shape[1]), F32),
            jax.ShapeDtypeStruct((T, wb.shape[1]), BF16),
        ],
        scratch_shapes=[pltpu.VMEM((tm, D_MODEL), BF16)],
        compiler_params=pltpu.CompilerParams(
            dimension_semantics=("arbitrary", "arbitrary"),
            vmem_limit_bytes=VMEM_LIMIT),
        name="inproj",
    )(h, g, wf, wb)


def _conv_kernel(av_p, av_c, av_n, ag_p, ag_c, ag_n, az_c, cw_ref, cb_ref, lg_ref, lb_ref,
                 o_ref, a_ext, *, ts, rc):
    i = pl.program_id(1)
    n = pl.num_programs(1)
    zero = jnp.zeros((HALO, W_CONV), F32)
    prev = jnp.where(i > 0, av_p[0] * _sigmoid(ag_p[0]), zero)
    cur = av_c[0] * _sigmoid(ag_c[0])
    nxt = jnp.where(i < n - 1, av_n[0] * _sigmoid(ag_n[0]), zero)
    for c in range(N_SLAB):
        cs = slice(c * LANES, (c + 1) * LANES)
        a_ext[c, 0:HALO, :] = prev[:, cs]
        a_ext[c, HALO:HALO + ts, :] = cur[:, cs]
        a_ext[c, HALO + ts:, :] = nxt[:, cs]

    def chunk(r, carry):
        r0 = pl.multiple_of(r * rc, rc)
        slabs = []
        for c in range(N_SLAB):
            cs = slice(c * LANES, (c + 1) * LANES)
            acc = jnp.broadcast_to(cb_ref[:, cs], (rc, LANES))
            for k in range(CONV_WIDTH):
                win = a_ext[c, pl.ds(r0 + (HALO - CONV_PAD) + k, rc, stride=1), :]
                acc = acc + win * cw_ref[c, k:k + 1, :]
            slabs.append(acc)
        acc = jnp.concatenate(slabs, axis=1)
        mu = jnp.mean(acc, axis=-1, keepdims=True)
        xc = acc - mu
        var = jnp.mean(xc * xc, axis=-1, keepdims=True)
        y = (xc * lax.rsqrt(var + EPS)) * lg_ref[...] + lb_ref[...]
        y = _silu(y) * _silu(az_c[0, pl.ds(r0, rc), :])
        o_ref[0, pl.ds(r0, rc), :] = y.astype(BF16)
        return carry

    lax.fori_loop(0, ts // rc, chunk, 0)


def _conv_branch(uf3, cw, cb, lg, lb, *, ts=512, rc=32):
    B, S, _ = uf3.shape
    nh = ts // HALO
    last_h = S // HALO - 1

    def cur(c):
        return pl.BlockSpec((1, ts, W_CONV), lambda b, i: (b, i, c))

    def prev(c):
        return pl.BlockSpec((1, HALO, W_CONV), lambda b, i: (b, jnp.maximum(i * nh - 1, 0), c))

    def nxt(c):
        return pl.BlockSpec((1, HALO, W_CONV), lambda b, i: (b, jnp.minimum((i + 1) * nh, last_h), c))

    def const(shape):
        return pl.BlockSpec(shape, lambda b, i: (0, 0))

    return pl.pallas_call(
        functools.partial(_conv_kernel, ts=ts, rc=rc),
        grid=(B, S // ts),
        in_specs=[prev(0), cur(0), nxt(0), prev(1), cur(1), nxt(1), cur(2),
                  pl.BlockSpec((N_SLAB, CONV_WIDTH, LANES), lambda b, i: (0, 0, 0)),
                  const((1, W_CONV)), const((1, W_CONV)), const((1, W_CONV))],
        out_specs=pl.BlockSpec((1, ts, W_CONV), lambda b, i: (b, i, 0)),
        out_shape=jax.ShapeDtypeStruct((B, S, W_CONV), BF16),
        scratch_shapes=[pltpu.VMEM((N_SLAB, ts + 2 * HALO, LANES), F32)],
        compiler_params=pltpu.CompilerParams(
            dimension_semantics=("arbitrary", "arbitrary"),
            vmem_limit_bytes=VMEM_LIMIT),
        name="conv_branch",
    )(uf3, uf3, uf3, uf3, uf3, uf3, uf3, cw, cb, lg, lb)


def _attn_kernel(sink_ref, q_ref, kp_ref, kc_ref, kn_ref, vp_ref, vc_ref, vn_ref, bz_ref,
                 bias_ref, o_ref):
    nk = 3 * BLK
    lane = lax.broadcasted_iota(jnp.int32, (nk, LANES), 1)
    lo = lane < HEAD_DIM
    lane_q = lax.broadcasted_iota(jnp.int32, (BLK, LANES), 1)
    lo_q = lane_q < HEAD_DIM
    zeros = jnp.zeros((nk, LANES), BF16)
    for h in range(N_KV_HEADS):
        cs = slice(h * LANES, (h + 1) * LANES)
        kd = jnp.concatenate([kp_ref[0, :, cs], kc_ref[0, :, cs], kn_ref[0, :, cs]], axis=0)
        vd = jnp.concatenate([vp_ref[0, :, cs], vc_ref[0, :, cs], vn_ref[0, :, cs]], axis=0)
        kblk = jnp.concatenate([jnp.where(lo, kd, zeros), jnp.where(lo, zeros, kd)], axis=0)
        vblk = jnp.concatenate([jnp.where(lo, vd, zeros), jnp.where(lo, zeros, vd)], axis=0)
        for p in range(N_PAIR):
            g = (h * N_PAIR + p)
            qs = slice(g * LANES, (g + 1) * LANES)
            q2 = q_ref[0, :, qs]
            s = lax.dot_general(q2, kblk, (((1,), (1,)), ((), ())),
                                preferred_element_type=F32)
            s = s + bias_ref[0, h, p]
            sa, sb = s[:, :nk], s[:, nk:]
            sink_a = sink_ref[2 * g]
            sink_b = sink_ref[2 * g + 1]
            ma = jnp.maximum(jnp.max(sa, axis=-1, keepdims=True), sink_a)
            mb = jnp.maximum(jnp.max(sb, axis=-1, keepdims=True), sink_b)
            ea = jnp.exp(sa - ma)
            eb = jnp.exp(sb - mb)
            da = jnp.sum(ea, axis=-1, keepdims=True) + jnp.exp(sink_a - ma)
            db = jnp.sum(eb, axis=-1, keepdims=True) + jnp.exp(sink_b - mb)
            pe = jnp.concatenate([ea, eb], axis=1).astype(BF16)
            o = jnp.dot(pe, vblk, preferred_element_type=F32)
            inv = jnp.where(lo_q, 1.0 / da, 1.0 / db)
            y = (o * inv) * _silu(bz_ref[0, :, qs])
            o_ref[0, :, qs] = y.astype(BF16)


def _attn_branch(sink, ub3, uf3, bias3):
    B, S, _ = ub3.shape
    nb = S // BLK
    kcol = W_ATT // W_KV_DUP
    vcol = kcol + 1

    def halo(col, d):
        return pl.BlockSpec((1, BLK, W_KV_DUP),
                            lambda b, j: (b, jnp.clip(j + d, 0, nb - 1), col))

    def variant(b, j):
        return (jnp.where(j == 0, 0, jnp.where(j == nb - 1, 2, 1)), 0, 0, 0, 0)

    return pl.pallas_call(
        _attn_kernel,
        grid=(B, nb),
        in_specs=[
            pl.BlockSpec(memory_space=pltpu.SMEM),
            pl.BlockSpec((1, BLK, W_ATT), lambda b, j: (b, j, 0)),
            halo(kcol, -1), halo(kcol, 0), halo(kcol, 1),
            halo(vcol, -1), halo(vcol, 0), halo(vcol, 1),
            pl.BlockSpec((1, BLK, W_ATT), lambda b, j: (b, j, 3)),
            pl.BlockSpec((1, N_KV_HEADS, N_PAIR, BLK, 6 * BLK), variant),
        ],
        out_specs=pl.BlockSpec((1, BLK, W_ATT), lambda b, j: (b, j, 0)),
        out_shape=jax.ShapeDtypeStruct((B, S, W_ATT), BF16),
        compiler_params=pltpu.CompilerParams(
            dimension_semantics=("arbitrary", "arbitrary"),
            vmem_limit_bytes=VMEM_LIMIT),
        name="attn_branch",
    )(sink, ub3, ub3, ub3, ub3, ub3, ub3, ub3, uf3, bias3)


def _outproj_kernel(h_ref, ya_ref, yb_ref, p_ref, woa_ref, wob_ref, wpe_ref, peg_ref, wpg_ref,
                    o_ref):
    h1 = h_ref[...] + jnp.dot(ya_ref[...], woa_ref[...], preferred_element_type=F32)
    h1 = h1 + jnp.dot(yb_ref[...], wob_ref[...], preferred_element_type=F32)
    e = jnp.dot(p_ref[...].astype(BF16), wpe_ref[...], preferred_element_type=F32)
    ms = jnp.mean(e * e, axis=-1, keepdims=True)
    e = (e * lax.rsqrt(ms + EPS)) * peg_ref[...]
    gate = jnp.dot(h1.astype(BF16), wpg_ref[...], preferred_element_type=F32)
    o_ref[...] = h1 + e * _sigmoid(gate)


def _outproj(h, ya, yb, p, woa, wob, wpe, peg, wpg, *, tm=256):
    T = h.shape[0]

    def row(width):
        return pl.BlockSpec((tm, width), lambda i: (i, 0))

    def const(shape):
        return pl.BlockSpec(shape, lambda i: (0, 0), pipeline_mode=pl.Buffered(1))

    return pl.pallas_call(
        _outproj_kernel,
        grid=(T // tm,),
        in_specs=[row(D_MODEL), row(W_CONV), row(W_ATT), row(PLE_DIM),
                  const((W_CONV, D_MODEL)), const((W_ATT, D_MODEL)), const((PLE_DIM, D_MODEL)),
                  const((1, D_MODEL)), const((D_MODEL, D_MODEL))],
        out_specs=row(D_MODEL),
        out_shape=jax.ShapeDtypeStruct((T, D_MODEL), F32),
        compiler_params=pltpu.CompilerParams(
            dimension_semantics=("arbitrary",),
            vmem_limit_bytes=VMEM_LIMIT),
        name="outproj",
    )(h, ya, yb, p, woa, wob, wpe, peg, wpg)


def _final_norm_kernel(h_ref, g_ref, o_ref):
    x = h_ref[...]
    ms = jnp.mean(x * x, axis=-1, keepdims=True)
    o_ref[...] = (x * lax.rsqrt(ms + EPS)) * g_ref[...]


def _final_norm(h, g, *, tm=1024):
    T = h.shape[0]
    return pl.pallas_call(
        _final_norm_kernel,
        grid=(T // tm,),
        in_specs=[pl.BlockSpec((tm, D_MODEL), lambda i: (i, 0)),
                  pl.BlockSpec((1, D_MODEL), lambda i: (0, 0))],
        out_specs=pl.BlockSpec((tm, D_MODEL), lambda i: (i, 0)),
        out_shape=jax.ShapeDtypeStruct((T, D_MODEL), F32),
        compiler_params=pltpu.CompilerParams(
            dimension_semantics=("arbitrary",),
            vmem_limit_bytes=VMEM_LIMIT),
        name="final_norm",
    )(h, g)


def _band_buckets():
    q_off = np.arange(BLK)[:, None]
    k_off = np.arange(3 * BLK)[None, :] - BLK
    rel = k_off - q_off
    half = NUM_BUCKETS // 2
    ret = (rel > 0).astype(np.int32) * half
    n = np.abs(rel)
    max_exact = half // 2
    large = max_exact + (np.log(np.maximum(n, 1) / max_exact)
                         / np.log(MAX_DISTANCE / max_exact)
                         * (half - max_exact)).astype(np.int32)
    large = np.minimum(large, half - 1)
    ret = ret + np.where(n < max_exact, n, large)
    return ret.astype(np.int32), (n <= WINDOW)


def _bias_table(rel_bias):
    buckets, band = _band_buckets()
    bias = jnp.take(rel_bias.astype(F32), jnp.asarray(buckets), axis=0)
    bias = jnp.transpose(bias, (2, 0, 1))
    bias = jnp.where(jnp.asarray(band)[None], bias, NEG)
    col = np.arange(3 * BLK)
    first = jnp.where(jnp.asarray(col < BLK)[None, None], NEG, bias)
    last = jnp.where(jnp.asarray(col >= 2 * BLK)[None, None], NEG, bias)
    tab = jnp.stack([first, bias, last])
    tab = tab.reshape(3, N_KV_HEADS, N_PAIR, 2, BLK, 3 * BLK)
    tab = jnp.transpose(tab, (0, 1, 2, 4, 3, 5))
    return tab.reshape(3, N_KV_HEADS, N_PAIR, BLK, 6 * BLK)


def _dup_heads(w):
    d = w.shape[0]
    w = w.reshape(d, N_KV_HEADS, 1, HEAD_DIM)
    return jnp.broadcast_to(w, (d, N_KV_HEADS, 2, HEAD_DIM)).reshape(d, W_KV_DUP)


def _prep_w_in(w):
    o = np.cumsum([W_CONV, W_CONV, W_CONV, W_ATT, W_KV, W_KV, W_ATT])
    a_val, a_glu, a_z = w[:, :o[0]], w[:, o[0]:o[1]], w[:, o[1]:o[2]]
    q, k, v, b_z = w[:, o[2]:o[3]], w[:, o[3]:o[4]], w[:, o[4]:o[5]], w[:, o[5]:o[6]]
    wf = jnp.concatenate([a_val, a_glu, a_z, b_z], axis=1).astype(BF16)
    wb = jnp.concatenate([q * (HEAD_DIM ** -0.5), _dup_heads(k), _dup_heads(v)],
                         axis=1).astype(BF16)
    return wf, wb


def kernel(x, p, norm_g, w_in, conv_w, conv_b, cln_g, cln_b, sink, rel_bias,
           w_out, w_pe, pe_g, w_pg, final_g):
    B, S, _ = x.shape
    T = B * S
    bias3 = _bias_table(rel_bias)
    h = x.reshape(T, D_MODEL)
    for i in range(DEPTH):
        wf, wb = _prep_w_in(w_in[i])
        uf, ub = _inproj(h, norm_g[i].reshape(1, D_MODEL), wf, wb)
        uf3 = uf.reshape(B, S, uf.shape[1])
        ub3 = ub.reshape(B, S, ub.shape[1])
        cw = jnp.transpose(conv_w[i].reshape(CONV_WIDTH, N_SLAB, LANES), (1, 0, 2))
        ya = _conv_branch(uf3, cw, conv_b[i].reshape(1, W_CONV),
                          cln_g[i].reshape(1, W_CONV), cln_b[i].reshape(1, W_CONV))
        yb = _attn_branch(sink[i], ub3, uf3, bias3)
        wo = w_out[i].astype(BF16)
        h = _outproj(h, ya.reshape(T, W_CONV), yb.reshape(T, W_ATT), p[i].reshape(T, PLE_DIM),
                     wo[:W_CONV], wo[W_CONV:], w_pe[i].astype(BF16),
                     pe_g[i].reshape(1, D_MODEL), w_pg[i].astype(BF16))
    out = _final_norm(h, final_g.reshape(1, D_MODEL))
    return out.reshape(B, S, D_MODEL)
```

```python
import functools

import numpy as np
import jax
import jax.numpy as jnp
from jax import lax
from jax.experimental import pallas as pl
from jax.experimental.pallas import tpu as pltpu

D_MODEL = 2048
DEPTH = 4
W_CONV = 1024
HEAD_DIM = 64
N_Q_HEADS = 16
N_KV_HEADS = 4
GQA_GROUP = 4
W_ATT = 1024
W_KV = 256
CONV_WIDTH = 31
CONV_PAD = 15
WINDOW = 128
BLK = 128
NUM_BUCKETS = 32
MAX_DISTANCE = 128
PLE_DIM = 256
EPS = 1e-6
NEG = -1e30

LANES = 128
SUBLANES = 8
HALO = 16
N_SLAB = W_CONV // LANES
N_GROUP = N_KV_HEADS
GW = W_ATT // N_GROUP
W_GROUP = 6 * GW
VMEM_LIMIT = 56 * 1024 * 1024

F32 = jnp.float32
BF16 = jnp.bfloat16


def _sigmoid(x):
    return jax.nn.sigmoid(x)


def _silu(x):
    return x * jax.nn.sigmoid(x)


def _rmsnorm(x, g):
    ms = jnp.mean(x * x, axis=-1, keepdims=True)
    return (x * lax.rsqrt(ms + EPS)) * g


def _inproj_kernel(x_ref, g_ref, w_ref, a_ref, gz_ref, gb_ref, q_ref, kv_ref, *scratch,
                   fuse_norm):
    if fuse_norm:
        (hn_ref,) = scratch

        @pl.when(pl.program_id(1) == 0)
        def _():
            hn_ref[...] = _rmsnorm(x_ref[...], g_ref[...]).astype(BF16)

        hn = hn_ref[...]
    else:
        hn = x_ref[...]

    def part(c):
        return jnp.dot(hn, w_ref[:, c * GW:(c + 1) * GW], preferred_element_type=F32)

    a = part(0) * _sigmoid(part(1))
    for s in range(GW // LANES):
        a_ref[s] = a[:, s * LANES:(s + 1) * LANES]
    gz_ref[...] = _silu(part(2))
    gb_ref[...] = _silu(part(3))
    q_ref[...] = part(4).astype(BF16)
    kv_ref[...] = part(5).astype(BF16)


def _inproj(x, g, w, *, fuse_norm, tm=1024):
    T = x.shape[0]
    scratch = [pltpu.VMEM((tm, D_MODEL), BF16)] if fuse_norm else []

    def col(dtype):
        return (pl.BlockSpec((tm, GW), lambda i, j: (i, j)),
                jax.ShapeDtypeStruct((T, N_GROUP * GW), dtype))

    specs, shapes = zip(col(F32), col(F32), col(BF16), col(BF16))
    spg = GW // LANES
    return pl.pallas_call(
        functools.partial(_inproj_kernel, fuse_norm=fuse_norm),
        grid=(T // tm, N_GROUP),
        in_specs=[
            pl.BlockSpec((tm, D_MODEL), lambda i, j: (i, 0)),
            pl.BlockSpec((1, D_MODEL), lambda i, j: (0, 0)),
            pl.BlockSpec((D_MODEL, W_GROUP), lambda i, j: (0, j)),
        ],
        out_specs=[pl.BlockSpec((spg, tm, LANES), lambda i, j: (j, i, 0))] + list(specs),
        out_shape=[jax.ShapeDtypeStruct((N_SLAB, T, LANES), F32)] + list(shapes),
        scratch_shapes=scratch,
        compiler_params=pltpu.CompilerParams(
            dimension_semantics=("arbitrary", "arbitrary"),
            vmem_limit_bytes=VMEM_LIMIT),
        name="inproj_norm" if fuse_norm else "inproj",
    )(x, g, w)


def _conv_kernel(ap_ref, ac_ref, an_ref, gz_ref, cw_ref, cb_ref, lg_ref, lb_ref,
                 o_ref, a_ext, *, ts, rc):
    i = pl.program_id(1)
    n = pl.num_programs(1)
    zero = jnp.zeros((N_SLAB, HALO, LANES), F32)
    a_ext[:, 0:HALO, :] = jnp.where(i > 0, ap_ref[:, 0], zero)
    a_ext[:, HALO:HALO + ts, :] = ac_ref[:, 0]
    a_ext[:, HALO + ts:, :] = jnp.where(i < n - 1, an_ref[:, 0], zero)

    n_sub = rc // SUBLANES
    first = HALO - CONV_PAD
    n_off = CONV_WIDTH + (n_sub - 1) * SUBLANES

    def chunk(r, carry):
        r0 = pl.multiple_of(r * rc, rc)
        ys = []
        for c in range(N_SLAB):
            cs = slice(c * LANES, (c + 1) * LANES)
            accs = [jnp.broadcast_to(cb_ref[:, cs], (SUBLANES, LANES))] * n_sub
            for o in range(n_off):
                win = a_ext[c, pl.ds(r0 + first + o, SUBLANES, stride=1), :]
                for j in range(n_sub):
                    k = o - j * SUBLANES
                    if 0 <= k < CONV_WIDTH:
                        accs[j] = accs[j] + win * cw_ref[c, k:k + 1, :]
            ys.append(jnp.concatenate(accs, axis=0))
        tot = ys[0]
        for c in range(1, N_SLAB):
            tot = tot + ys[c]
        mu = jnp.sum(tot, axis=-1, keepdims=True) * (1.0 / W_CONV)
        xcs = [y - mu for y in ys]
        sq = xcs[0] * xcs[0]
        for c in range(1, N_SLAB):
            sq = sq + xcs[c] * xcs[c]
        var = jnp.sum(sq, axis=-1, keepdims=True) * (1.0 / W_CONV)
        rs = lax.rsqrt(var + EPS)
        for c in range(N_SLAB):
            cs = slice(c * LANES, (c + 1) * LANES)
            y = (xcs[c] * rs) * lg_ref[:, cs] + lb_ref[:, cs]
            y = _silu(y) * gz_ref[0, pl.ds(r0, rc), cs]
            o_ref[0, pl.ds(r0, rc), cs] = y.astype(BF16)
        return carry

    lax.fori_loop(0, ts // rc, chunk, 0)


def _conv_branch(a4, gz3, cw, cb, lg, lb, *, ts=512, rc=32):
    _, B, S, _ = a4.shape
    nh = ts // HALO
    last_h = S // HALO - 1

    def const(shape):
        return pl.BlockSpec(shape, lambda b, i: (0,) * len(shape))

    return pl.pallas_call(
        functools.partial(_conv_kernel, ts=ts, rc=rc),
        grid=(B, S // ts),
        in_specs=[
            pl.BlockSpec((N_SLAB, 1, HALO, LANES),
                         lambda b, i: (0, b, jnp.maximum(i * nh - 1, 0), 0)),
            pl.BlockSpec((N_SLAB, 1, ts, LANES), lambda b, i: (0, b, i, 0)),
            pl.BlockSpec((N_SLAB, 1, HALO, LANES),
                         lambda b, i: (0, b, jnp.minimum((i + 1) * nh, last_h), 0)),
            pl.BlockSpec((1, ts, W_CONV), lambda b, i: (b, i, 0)),
            const((N_SLAB, CONV_WIDTH, LANES)),
            const((1, W_CONV)), const((1, W_CONV)), const((1, W_CONV))],
        out_specs=pl.BlockSpec((1, ts, W_CONV), lambda b, i: (b, i, 0)),
        out_shape=jax.ShapeDtypeStruct((B, S, W_CONV), BF16),
        scratch_shapes=[pltpu.VMEM((N_SLAB, ts + 2 * HALO, LANES), F32)],
        compiler_params=pltpu.CompilerParams(
            dimension_semantics=("arbitrary", "arbitrary"),
            vmem_limit_bytes=VMEM_LIMIT),
        name="conv_branch",
    )(a4, a4, a4, gz3, cw, cb, lg, lb)


def _attn_kernel(sink_ref, q_ref, kvp_ref, kvc_ref, kvn_ref, gb_ref, bias_ref, o_ref):
    nk = 3 * BLK
    lo = lax.broadcasted_iota(jnp.int32, (nk, LANES), 1) < HEAD_DIM
    lo_q = lax.broadcasted_iota(jnp.int32, (2 * BLK, LANES), 1) < HEAD_DIM
    top = lax.broadcasted_iota(jnp.int32, (2 * BLK, 1), 0) < BLK
    zeros = jnp.zeros((nk, LANES), BF16)
    for h in range(N_KV_HEADS):
        ks = slice(h * GW, h * GW + LANES)
        vs = slice(h * GW + LANES, (h + 1) * GW)
        kd = jnp.concatenate([kvp_ref[0, :, ks], kvc_ref[0, :, ks], kvn_ref[0, :, ks]], axis=0)
        vd = jnp.concatenate([kvp_ref[0, :, vs], kvc_ref[0, :, vs], kvn_ref[0, :, vs]], axis=0)
        kblk = jnp.concatenate([jnp.where(lo, kd, zeros), jnp.where(lo, zeros, kd)], axis=0)
        vblk = jnp.concatenate([jnp.where(lo, vd, zeros), jnp.where(lo, zeros, vd)], axis=0)
        hs = slice(h * GW, (h + 1) * GW)
        qh = q_ref[0, :, hs]
        lhs = jnp.concatenate([qh[:, :LANES], qh[:, LANES:]], axis=0)
        s = lax.dot_general(lhs, kblk, (((1,), (1,)), ((), ())), preferred_element_type=F32)
        s = s + bias_ref[0, h]
        sa, sb = s[:, :nk], s[:, nk:]
        g0 = h * GQA_GROUP
        sk_a = jnp.where(top, sink_ref[g0], sink_ref[g0 + 2])
        sk_b = jnp.where(top, sink_ref[g0 + 1], sink_ref[g0 + 3])
        ma = jnp.maximum(jnp.max(sa, axis=-1, keepdims=True), sk_a)
        mb = jnp.maximum(jnp.max(sb, axis=-1, keepdims=True), sk_b)
        ea = jnp.exp(sa - ma)
        eb = jnp.exp(sb - mb)
        da = jnp.sum(ea, axis=-1, keepdims=True) + jnp.exp(sk_a - ma)
        db = jnp.sum(eb, axis=-1, keepdims=True) + jnp.exp(sk_b - mb)
        pe = jnp.concatenate([ea, eb], axis=1).astype(BF16)
        o = jnp.dot(pe, vblk, preferred_element_type=F32)
        y = o * jnp.where(lo_q, 1.0 / da, 1.0 / db)
        yh = jnp.concatenate([y[:BLK], y[BLK:]], axis=1)
        o_ref[0, :, hs] = (yh * gb_ref[0, :, hs]).astype(BF16)


def _attn_branch(sink, q3, kv3, gb3, bias4):
    B, S, _ = q3.shape
    nb = S // BLK

    def row(d):
        return pl.BlockSpec((1, BLK, W_ATT), lambda b, j: (b, jnp.clip(j + d, 0, nb - 1), 0))

    def variant(b, j):
        return (jnp.where(j == 0, 0, jnp.where(j == nb - 1, 2, 1)), 0, 0, 0)

    return pl.pallas_call(
        _attn_kernel,
        grid=(B, nb),
        in_specs=[
            pl.BlockSpec(memory_space=pltpu.SMEM),
            row(0), row(-1), row(0), row(1), row(0),
            pl.BlockSpec((1, N_KV_HEADS, 2 * BLK, 6 * BLK), variant),
        ],
        out_specs=row(0),
        out_shape=jax.ShapeDtypeStruct((B, S, W_ATT), BF16),
        compiler_params=pltpu.CompilerParams(
            dimension_semantics=("arbitrary", "arbitrary"),
            vmem_limit_bytes=VMEM_LIMIT),
        name="attn_branch",
    )(sink, q3, kv3, kv3, kv3, gb3, bias4)


def _outproj_kernel(h_ref, ya_ref, yb_ref, p_ref, woa_ref, wob_ref, wpe_ref, peg_ref, wpg_ref,
                    ng_ref, *out_refs, last):
    h1 = h_ref[...] + jnp.dot(ya_ref[...], woa_ref[...], preferred_element_type=F32)
    h1 = h1 + jnp.dot(yb_ref[...], wob_ref[...], preferred_element_type=F32)
    e = jnp.dot(p_ref[...].astype(BF16), wpe_ref[...], preferred_element_type=F32)
    e = _rmsnorm(e, peg_ref[...])
    gate = jnp.dot(h1.astype(BF16), wpg_ref[...], preferred_element_type=F32)
    h2 = h1 + e * _sigmoid(gate)
    if last:
        (o_ref,) = out_refs
        o_ref[...] = _rmsnorm(h2, ng_ref[...])
    else:
        h_out, hn_out = out_refs
        h_out[...] = h2
        hn_out[...] = _rmsnorm(h2, ng_ref[...]).astype(BF16)


def _outproj(h, ya, yb, p, woa, wob, wpe, peg, wpg, ng, *, last, tm=256):
    T = h.shape[0]

    def row(width):
        return pl.BlockSpec((tm, width), lambda i: (i, 0))

    def const(shape):
        return pl.BlockSpec(shape, lambda i: (0, 0), pipeline_mode=pl.Buffered(1))

    if last:
        out_specs = [row(D_MODEL)]
        out_shape = [jax.ShapeDtypeStruct((T, D_MODEL), F32)]
    else:
        out_specs = [row(D_MODEL), row(D_MODEL)]
        out_shape = [jax.ShapeDtypeStruct((T, D_MODEL), F32),
                     jax.ShapeDtypeStruct((T, D_MODEL), BF16)]
    return pl.pallas_call(
        functools.partial(_outproj_kernel, last=last),
        grid=(T // tm,),
        in_specs=[row(D_MODEL), row(W_CONV), row(W_ATT), row(PLE_DIM),
                  const((W_CONV, D_MODEL)), const((W_ATT, D_MODEL)), const((PLE_DIM, D_MODEL)),
                  const((1, D_MODEL)), const((D_MODEL, D_MODEL)), const((1, D_MODEL))],
        out_specs=out_specs,
        out_shape=out_shape,
        compiler_params=pltpu.CompilerParams(
            dimension_semantics=("arbitrary",),
            vmem_limit_bytes=VMEM_LIMIT),
        name="outproj_final" if last else "outproj",
    )(h, ya, yb, p, woa, wob, wpe, peg, wpg, ng)


def _band_buckets():
    q_off = np.arange(BLK)[:, None]
    k_off = np.arange(3 * BLK)[None, :] - BLK
    rel = k_off - q_off
    half = NUM_BUCKETS // 2
    ret = (rel > 0).astype(np.int32) * half
    n = np.abs(rel)
    max_exact = half // 2
    large = max_exact + (np.log(np.maximum(n, 1) / max_exact)
                         / np.log(MAX_DISTANCE / max_exact)
                         * (half - max_exact)).astype(np.int32)
    large = np.minimum(large, half - 1)
    ret = ret + np.where(n < max_exact, n, large)
    return ret.astype(np.int32), (n <= WINDOW)


def _bias_table(rel_bias):
    buckets, band = _band_buckets()
    onehot = np.zeros((BLK * 3 * BLK, NUM_BUCKETS), np.float32)
    onehot[np.arange(onehot.shape[0]), buckets.reshape(-1)] = 1.0
    bias = jnp.dot(jnp.asarray(onehot), rel_bias.astype(F32),
                   precision=lax.Precision.HIGHEST)
    bias = jnp.transpose(bias.reshape(BLK, 3 * BLK, N_Q_HEADS), (2, 0, 1))
    bias = jnp.where(jnp.asarray(band)[None], bias, NEG)
    col = np.arange(3 * BLK)
    first = jnp.where(jnp.asarray(col < BLK)[None, None], NEG, bias)
    last = jnp.where(jnp.asarray(col >= 2 * BLK)[None, None], NEG, bias)
    tab = jnp.stack([first, bias, last])
    tab = tab.reshape(3, N_KV_HEADS, 2, 2, BLK, 3 * BLK)
    tab = jnp.transpose(tab, (0, 1, 2, 4, 3, 5))
    return tab.reshape(3, N_KV_HEADS, 2 * BLK, 6 * BLK)


def _prep_w_in(w_in):
    o = np.cumsum([W_CONV, W_CONV, W_CONV, W_ATT, W_KV, W_KV, W_ATT])
    a_val, a_glu, a_z = w_in[..., :o[0]], w_in[..., o[0]:o[1]], w_in[..., o[1]:o[2]]
    q, k, v, b_z = (w_in[..., o[2]:o[3]], w_in[..., o[3]:o[4]], w_in[..., o[4]:o[5]],
                    w_in[..., o[5]:o[6]])
    parts = []
    for j in range(N_GROUP):
        g = slice(j * GW, (j + 1) * GW)
        hd = slice(j * HEAD_DIM, (j + 1) * HEAD_DIM)
        parts += [a_val[..., g], a_glu[..., g], a_z[..., g], b_z[..., g],
                  q[..., g] * (HEAD_DIM ** -0.5), k[..., hd], k[..., hd], v[..., hd], v[..., hd]]
    return jnp.concatenate(parts, axis=-1).astype(BF16)


def kernel(x, p, norm_g, w_in, conv_w, conv_b, cln_g, cln_b, sink, rel_bias,
           w_out, w_pe, pe_g, w_pg, final_g):
    B, S, _ = x.shape
    T = B * S
    bias4 = _bias_table(rel_bias)
    w_all = _prep_w_in(w_in)
    wo = w_out.astype(BF16)
    wpe = w_pe.astype(BF16)
    wpg = w_pg.astype(BF16)
    cw = jnp.transpose(conv_w.reshape(DEPTH, CONV_WIDTH, N_SLAB, LANES), (0, 2, 1, 3))
    h = x.reshape(T, D_MODEL)
    hn = h
    for i in range(DEPTH):
        a, gz, gb, q, kv = _inproj(hn, norm_g[i].reshape(1, D_MODEL), w_all[i], fuse_norm=(i == 0))
        ya = _conv_branch(a.reshape(N_SLAB, B, S, LANES), gz.reshape(B, S, W_CONV), cw[i],
                          conv_b[i].reshape(1, W_CONV), cln_g[i].reshape(1, W_CONV),
                          cln_b[i].reshape(1, W_CONV))
        yb = _attn_branch(sink[i], q.reshape(B, S, W_ATT), kv.reshape(B, S, W_ATT),
                          gb.reshape(B, S, W_ATT), bias4)
        last = i == DEPTH - 1
        ng = final_g if last else norm_g[i + 1]
        outs = _outproj(h, ya.reshape(T, W_CONV), yb.reshape(T, W_ATT), p[i].reshape(T, PLE_DIM),
                        wo[i, :W_CONV], wo[i, W_CONV:], wpe[i], pe_g[i].reshape(1, D_MODEL),
                        wpg[i], ng.reshape(1, D_MODEL), last=last)
        if last:
            (h,) = outs
        else:
            h, hn = outs
    return h.reshape(B, S, D_MODEL)
```

```python
import functools

import numpy as np
import jax
import jax.numpy as jnp
from jax import lax
from jax.experimental import pallas as pl
from jax.experimental.pallas import tpu as pltpu

D_MODEL = 2048
DEPTH = 4
W_CONV = 1024
HEAD_DIM = 64
N_Q_HEADS = 16
N_KV_HEADS = 4
GQA_GROUP = 4
W_ATT = 1024
W_KV = 256
CONV_WIDTH = 31
CONV_PAD = 15
WINDOW = 128
BLK = 128
NUM_BUCKETS = 32
MAX_DISTANCE = 128
PLE_DIM = 256
EPS = 1e-6
NEG = -1e30
LOG2E = float(np.log2(np.e))

LANES = 128
SUBLANES = 8
HALO = 16
N_SLAB = W_CONV // LANES
N_GROUP = N_KV_HEADS
GW = W_ATT // N_GROUP
W_GROUP = 6 * GW
VMEM_LIMIT = 56 * 1024 * 1024

F32 = jnp.float32
BF16 = jnp.bfloat16


def _sigmoid(x):
    return jax.nn.sigmoid(x)


def _silu(x):
    return x * jax.nn.sigmoid(x)


def _rmsnorm(x, g):
    ms = jnp.mean(x * x, axis=-1, keepdims=True)
    return (x * lax.rsqrt(ms + EPS)) * g


def _inproj_kernel(x_ref, g_ref, w_ref, a_ref, gz_ref, gb_ref, q_ref, kv_ref, *scratch,
                   fuse_norm):
    if fuse_norm:
        (hn_ref,) = scratch

        @pl.when(pl.program_id(1) == 0)
        def _():
            hn_ref[...] = _rmsnorm(x_ref[...], g_ref[...]).astype(BF16)

        hn = hn_ref[...]
    else:
        hn = x_ref[...]

    def part(c):
        return jnp.dot(hn, w_ref[:, c * GW:(c + 1) * GW], preferred_element_type=F32)

    a = part(0) * _sigmoid(part(1))
    for s in range(GW // LANES):
        a_ref[s] = a[:, s * LANES:(s + 1) * LANES]
    gz_ref[...] = _silu(part(2))
    gb_ref[...] = _silu(part(3))
    q_ref[...] = part(4).astype(BF16)
    kv_ref[...] = part(5).astype(BF16)


def _inproj(x, g, w, *, fuse_norm, tm=1024):
    T = x.shape[0]
    scratch = [pltpu.VMEM((tm, D_MODEL), BF16)] if fuse_norm else []

    def col(dtype):
        return (pl.BlockSpec((tm, GW), lambda i, j: (i, j)),
                jax.ShapeDtypeStruct((T, N_GROUP * GW), dtype))

    specs, shapes = zip(col(F32), col(F32), col(BF16), col(BF16))
    spg = GW // LANES
    return pl.pallas_call(
        functools.partial(_inproj_kernel, fuse_norm=fuse_norm),
        grid=(T // tm, N_GROUP),
        in_specs=[
            pl.BlockSpec((tm, D_MODEL), lambda i, j: (i, 0)),
            pl.BlockSpec((1, D_MODEL), lambda i, j: (0, 0)),
            pl.BlockSpec((D_MODEL, W_GROUP), lambda i, j: (0, j)),
        ],
        out_specs=[pl.BlockSpec((spg, tm, LANES), lambda i, j: (j, i, 0))] + list(specs),
        out_shape=[jax.ShapeDtypeStruct((N_SLAB, T, LANES), F32)] + list(shapes),
        scratch_shapes=scratch,
        compiler_params=pltpu.CompilerParams(
            dimension_semantics=("arbitrary", "arbitrary"),
            vmem_limit_bytes=VMEM_LIMIT),
        name="inproj_norm" if fuse_norm else "inproj",
    )(x, g, w)


def _conv_kernel(ap_ref, ac_ref, an_ref, gz_ref, cw_ref, cb_ref, lg_ref, lb_ref,
                 o_ref, a_ext, y_buf, *, ts, rc, rn):
    i = pl.program_id(1)
    n = pl.num_programs(1)
    zero = jnp.zeros((N_SLAB, HALO, LANES), F32)
    a_ext[:, 0:HALO, :] = jnp.where(i > 0, ap_ref[:, 0], zero)
    a_ext[:, HALO:HALO + ts, :] = ac_ref[:, 0]
    a_ext[:, HALO + ts:, :] = jnp.where(i < n - 1, an_ref[:, 0], zero)

    n_sub = rc // SUBLANES
    first = HALO - CONV_PAD
    n_off = CONV_WIDTH + (n_sub - 1) * SUBLANES

    for c in range(N_SLAB):
        taps = [jnp.broadcast_to(cw_ref[c, k:k + 1, :], (SUBLANES, LANES))
                for k in range(CONV_WIDTH)]
        bias = jnp.broadcast_to(cb_ref[c], (SUBLANES, LANES))

        def conv_chunk(r, carry, c=c, taps=taps, bias=bias):
            r0 = pl.multiple_of(r * rc, rc)
            accs = [[bias, None] for _ in range(n_sub)]
            for o in range(n_off):
                win = a_ext[c, pl.ds(r0 + first + o, SUBLANES, stride=1), :]
                for j in range(n_sub):
                    k = o - j * SUBLANES
                    if 0 <= k < CONV_WIDTH:
                        prod = win * taps[k]
                        cur = accs[j][k % 2]
                        accs[j][k % 2] = prod if cur is None else cur + prod
            y_buf[c, pl.ds(r0, rc), :] = jnp.concatenate([e + o_ for e, o_ in accs], axis=0)
            return carry

        lax.fori_loop(0, ts // rc, conv_chunk, 0)

    def chunk(r, carry):
        r0 = pl.multiple_of(r * rn, rn)
        ys = [y_buf[c, pl.ds(r0, rn), :] for c in range(N_SLAB)]
        tot = ys[0]
        for c in range(1, N_SLAB):
            tot = tot + ys[c]
        mu = jnp.sum(tot, axis=-1, keepdims=True) * (1.0 / W_CONV)
        xcs = [y - mu for y in ys]
        sq = xcs[0] * xcs[0]
        for c in range(1, N_SLAB):
            sq = sq + xcs[c] * xcs[c]
        var = jnp.sum(sq, axis=-1, keepdims=True) * (1.0 / W_CONV)
        rs = lax.rsqrt(var + EPS)
        for c in range(N_SLAB):
            cs = slice(c * LANES, (c + 1) * LANES)
            y = (xcs[c] * rs) * lg_ref[:, cs] + lb_ref[:, cs]
            y = _silu(y) * gz_ref[0, pl.ds(r0, rn), cs]
            o_ref[0, pl.ds(r0, rn), cs] = y.astype(BF16)
        return carry

    lax.fori_loop(0, ts // rn, chunk, 0, unroll=4)


def _conv_branch(a4, gz3, cw, cb, lg, lb, *, ts=512, rc=128, rn=32):
    _, B, S, _ = a4.shape
    nh = ts // HALO
    last_h = S // HALO - 1

    def const(shape):
        return pl.BlockSpec(shape, lambda b, i: (0,) * len(shape))

    return pl.pallas_call(
        functools.partial(_conv_kernel, ts=ts, rc=rc, rn=rn),
        grid=(B, S // ts),
        in_specs=[
            pl.BlockSpec((N_SLAB, 1, HALO, LANES),
                         lambda b, i: (0, b, jnp.maximum(i * nh - 1, 0), 0)),
            pl.BlockSpec((N_SLAB, 1, ts, LANES), lambda b, i: (0, b, i, 0)),
            pl.BlockSpec((N_SLAB, 1, HALO, LANES),
                         lambda b, i: (0, b, jnp.minimum((i + 1) * nh, last_h), 0)),
            pl.BlockSpec((1, ts, W_CONV), lambda b, i: (b, i, 0)),
            const((N_SLAB, CONV_WIDTH, LANES)), const((N_SLAB, 1, LANES)),
            const((1, W_CONV)), const((1, W_CONV))],
        out_specs=pl.BlockSpec((1, ts, W_CONV), lambda b, i: (b, i, 0)),
        out_shape=jax.ShapeDtypeStruct((B, S, W_CONV), BF16),
        scratch_shapes=[pltpu.VMEM((N_SLAB, ts + 2 * HALO, LANES), F32),
                        pltpu.VMEM((N_SLAB, ts, LANES), F32)],
        compiler_params=pltpu.CompilerParams(
            dimension_semantics=("arbitrary", "arbitrary"),
            vmem_limit_bytes=VMEM_LIMIT),
        name="conv_branch",
    )(a4, a4, a4, gz3, cw, cb, lg, lb)


def _attn_kernel(sink_ref, q_ref, kvp_ref, kvc_ref, kvn_ref, gb_ref, bias_ref, o_ref):
    nk = 3 * BLK
    lo = lax.broadcasted_iota(jnp.int32, (nk, LANES), 1) < HEAD_DIM
    lo_q = lax.broadcasted_iota(jnp.int32, (2 * BLK, LANES), 1) < HEAD_DIM
    top = lax.broadcasted_iota(jnp.int32, (2 * BLK, 1), 0) < BLK
    zeros = jnp.zeros((nk, LANES), BF16)
    ones_lo = jnp.where(lo, 1.0, 0.0).astype(BF16)
    ones_hi = jnp.where(lo, 0.0, 1.0).astype(BF16)
    sum_cols = jnp.concatenate([ones_lo, ones_hi], axis=0)
    for h in range(N_KV_HEADS):
        ks = slice(h * GW, h * GW + LANES)
        vs = slice(h * GW + LANES, (h + 1) * GW)
        kd = jnp.concatenate([kvp_ref[0, :, ks], kvc_ref[0, :, ks], kvn_ref[0, :, ks]], axis=0)
        vd = jnp.concatenate([kvp_ref[0, :, vs], kvc_ref[0, :, vs], kvn_ref[0, :, vs]], axis=0)
        kblk = jnp.concatenate([jnp.where(lo, kd, zeros), jnp.where(lo, zeros, kd)], axis=0)
        vblk = jnp.concatenate([jnp.where(lo, vd, zeros), jnp.where(lo, zeros, vd)], axis=0)
        vblk = jnp.concatenate([vblk, sum_cols], axis=1)
        hs = slice(h * GW, (h + 1) * GW)
        qh = q_ref[0, :, hs]
        lhs = jnp.concatenate([qh[:, :LANES], qh[:, LANES:]], axis=0)
        s = lax.dot_general(lhs, kblk, (((1,), (1,)), ((), ())), preferred_element_type=F32)
        s = s + bias_ref[0, h]
        sa, sb = s[:, :nk], s[:, nk:]
        g0 = h * GQA_GROUP
        sk_a = jnp.where(top, sink_ref[g0], sink_ref[g0 + 2])
        sk_b = jnp.where(top, sink_ref[g0 + 1], sink_ref[g0 + 3])
        ma = jnp.maximum(jnp.max(sa, axis=-1, keepdims=True), sk_a)
        mb = jnp.maximum(jnp.max(sb, axis=-1, keepdims=True), sk_b)
        pe = jnp.concatenate([jnp.exp2(sa - ma), jnp.exp2(sb - mb)], axis=1).astype(BF16)
        o = jnp.dot(pe, vblk, preferred_element_type=F32)
        den = o[:, LANES:] + jnp.where(lo_q, jnp.exp2(sk_a - ma), jnp.exp2(sk_b - mb))
        y = o[:, :LANES] * (1.0 / den)
        yh = jnp.concatenate([y[:BLK], y[BLK:]], axis=1)
        o_ref[0, :, hs] = (yh * gb_ref[0, :, hs]).astype(BF16)


def _attn_branch(sink, q3, kv3, gb3, bias4):
    B, S, _ = q3.shape
    nb = S // BLK

    def row(d):
        return pl.BlockSpec((1, BLK, W_ATT), lambda b, j: (b, jnp.clip(j + d, 0, nb - 1), 0))

    def variant(b, j):
        return (jnp.where(j == 0, 0, jnp.where(j == nb - 1, 2, 1)), 0, 0, 0)

    return pl.pallas_call(
        _attn_kernel,
        grid=(B, nb),
        in_specs=[
            pl.BlockSpec(memory_space=pltpu.SMEM),
            row(0), row(-1), row(0), row(1), row(0),
            pl.BlockSpec((1, N_KV_HEADS, 2 * BLK, 6 * BLK), variant),
        ],
        out_specs=row(0),
        out_shape=jax.ShapeDtypeStruct((B, S, W_ATT), BF16),
        compiler_params=pltpu.CompilerParams(
            dimension_semantics=("arbitrary", "arbitrary"),
            vmem_limit_bytes=VMEM_LIMIT),
        name="attn_branch",
    )(sink, q3, kv3, kv3, kv3, gb3, bias4)


def _outproj_kernel(h_ref, ya_ref, yb_ref, p_ref, woa_ref, wob_ref, wpe_ref, peg_ref, wpg_ref,
                    ng_ref, *out_refs, last):
    h1 = h_ref[...] + jnp.dot(ya_ref[...], woa_ref[...], preferred_element_type=F32)
    h1 = h1 + jnp.dot(yb_ref[...], wob_ref[...], preferred_element_type=F32)
    e = jnp.dot(p_ref[...].astype(BF16), wpe_ref[...], preferred_element_type=F32)
    e = _rmsnorm(e, peg_ref[...])
    gate = jnp.dot(h1.astype(BF16), wpg_ref[...], preferred_element_type=F32)
    h2 = h1 + e * _sigmoid(gate)
    if last:
        (o_ref,) = out_refs
        o_ref[...] = _rmsnorm(h2, ng_ref[...])
    else:
        h_out, hn_out = out_refs
        h_out[...] = h2
        hn_out[...] = _rmsnorm(h2, ng_ref[...]).astype(BF16)


def _outproj(h, ya, yb, p, woa, wob, wpe, peg, wpg, ng, *, last, tm=256):
    T = h.shape[0]

    def row(width):
        return pl.BlockSpec((tm, width), lambda i: (i, 0))

    def const(shape):
        return pl.BlockSpec(shape, lambda i: (0, 0), pipeline_mode=pl.Buffered(1))

    if last:
        out_specs = [row(D_MODEL)]
        out_shape = [jax.ShapeDtypeStruct((T, D_MODEL), F32)]
    else:
        out_specs = [row(D_MODEL), row(D_MODEL)]
        out_shape = [jax.ShapeDtypeStruct((T, D_MODEL), F32),
                     jax.ShapeDtypeStruct((T, D_MODEL), BF16)]
    return pl.pallas_call(
        functools.partial(_outproj_kernel, last=last),
        grid=(T // tm,),
        in_specs=[row(D_MODEL), row(W_CONV), row(W_ATT), row(PLE_DIM),
                  const((W_CONV, D_MODEL)), const((W_ATT, D_MODEL)), const((PLE_DIM, D_MODEL)),
                  const((1, D_MODEL)), const((D_MODEL, D_MODEL)), const((1, D_MODEL))],
        out_specs=out_specs,
        out_shape=out_shape,
        compiler_params=pltpu.CompilerParams(
            dimension_semantics=("arbitrary",),
            vmem_limit_bytes=VMEM_LIMIT),
        name="outproj_final" if last else "outproj",
    )(h, ya, yb, p, woa, wob, wpe, peg, wpg, ng)


def _band_buckets():
    q_off = np.arange(BLK)[:, None]
    k_off = np.arange(3 * BLK)[None, :] - BLK
    rel = k_off - q_off
    half = NUM_BUCKETS // 2
    ret = (rel > 0).astype(np.int32) * half
    n = np.abs(rel)
    max_exact = half // 2
    large = max_exact + (np.log(np.maximum(n, 1) / max_exact)
                         / np.log(MAX_DISTANCE / max_exact)
                         * (half - max_exact)).astype(np.int32)
    large = np.minimum(large, half - 1)
    ret = ret + np.where(n < max_exact, n, large)
    return ret.astype(np.int32), (n <= WINDOW)


def _bias_table(rel_bias):
    buckets, band = _band_buckets()
    onehot = np.zeros((BLK * 3 * BLK, NUM_BUCKETS), np.float32)
    onehot[np.arange(onehot.shape[0]), buckets.reshape(-1)] = 1.0
    bias = jnp.dot(jnp.asarray(onehot), rel_bias.astype(F32),
                   precision=lax.Precision.HIGHEST)
    bias = jnp.transpose(bias.reshape(BLK, 3 * BLK, N_Q_HEADS), (2, 0, 1))
    bias = jnp.where(jnp.asarray(band)[None], bias * LOG2E, NEG)
    col = np.arange(3 * BLK)
    first = jnp.where(jnp.asarray(col < BLK)[None, None], NEG, bias)
    last = jnp.where(jnp.asarray(col >= 2 * BLK)[None, None], NEG, bias)
    tab = jnp.stack([first, bias, last])
    tab = tab.reshape(3, N_KV_HEADS, 2, 2, BLK, 3 * BLK)
    tab = jnp.transpose(tab, (0, 1, 2, 4, 3, 5))
    return tab.reshape(3, N_KV_HEADS, 2 * BLK, 6 * BLK)


def _prep_w_in(w_in):
    o = np.cumsum([W_CONV, W_CONV, W_CONV, W_ATT, W_KV, W_KV, W_ATT])
    a_val, a_glu, a_z = w_in[..., :o[0]], w_in[..., o[0]:o[1]], w_in[..., o[1]:o[2]]
    q, k, v, b_z = (w_in[..., o[2]:o[3]], w_in[..., o[3]:o[4]], w_in[..., o[4]:o[5]],
                    w_in[..., o[5]:o[6]])
    parts = []
    for j in range(N_GROUP):
        g = slice(j * GW, (j + 1) * GW)
        hd = slice(j * HEAD_DIM, (j + 1) * HEAD_DIM)
        parts += [a_val[..., g], a_glu[..., g], a_z[..., g], b_z[..., g],
                  q[..., g] * (HEAD_DIM ** -0.5 * LOG2E), k[..., hd], k[..., hd],
                  v[..., hd], v[..., hd]]
    return jnp.concatenate(parts, axis=-1).astype(BF16)


def kernel(x, p, norm_g, w_in, conv_w, conv_b, cln_g, cln_b, sink, rel_bias,
           w_out, w_pe, pe_g, w_pg, final_g):
    B, S, _ = x.shape
    T = B * S
    bias4 = _bias_table(rel_bias)
    w_all = _prep_w_in(w_in)
    wo = w_out.astype(BF16)
    wpe = w_pe.astype(BF16)
    wpg = w_pg.astype(BF16)
    cw = jnp.transpose(conv_w.reshape(DEPTH, CONV_WIDTH, N_SLAB, LANES), (0, 2, 1, 3))
    h = x.reshape(T, D_MODEL)
    hn = h
    for i in range(DEPTH):
        a, gz, gb, q, kv = _inproj(hn, norm_g[i].reshape(1, D_MODEL), w_all[i], fuse_norm=(i == 0))
        ya = _conv_branch(a.reshape(N_SLAB, B, S, LANES), gz.reshape(B, S, W_CONV), cw[i],
                          conv_b[i].reshape(N_SLAB, 1, LANES), cln_g[i].reshape(1, W_CONV),
                          cln_b[i].reshape(1, W_CONV))
        yb = _attn_branch(sink[i] * LOG2E, q.reshape(B, S, W_ATT), kv.reshape(B, S, W_ATT),
                          gb.reshape(B, S, W_ATT), bias4)
        last = i == DEPTH - 1
        ng = final_g if last else norm_g[i + 1]
        outs = _outproj(h, ya.reshape(T, W_CONV), yb.reshape(T, W_ATT), p[i].reshape(T, PLE_DIM),
                        wo[i, :W_CONV], wo[i, W_CONV:], wpe[i], pe_g[i].reshape(1, D_MODEL),
                        wpg[i], ng.reshape(1, D_MODEL), last=last)
        if last:
            (h,) = outs
        else:
            h, hn = outs
    return h.reshape(B, S, D_MODEL)
```

```python
import functools

import numpy as np
import jax
import jax.numpy as jnp
from jax import lax
from jax.experimental import pallas as pl
from jax.experimental.pallas import tpu as pltpu

D_MODEL = 2048
DEPTH = 4
W_CONV = 1024
HEAD_DIM = 64
N_Q_HEADS = 16
N_KV_HEADS = 4
GQA_GROUP = 4
W_ATT = 1024
W_KV = 256
CONV_WIDTH = 31
CONV_PAD = 15
WINDOW = 128
BLK = 128
NUM_BUCKETS = 32
MAX_DISTANCE = 128
PLE_DIM = 256
EPS = 1e-6
NEG = -1e30
LOG2E = float(np.log2(np.e))

LANES = 128
SUBLANES = 8
HALO = 16
N_SLAB = W_CONV // LANES
HW = GQA_GROUP * HEAD_DIM
MXU_N = 256
N_GROUP = 2
GW = W_ATT // N_GROUP
KVW = (2 * W_KV) // N_GROUP
W_GROUP = 5 * GW + KVW
assert W_GROUP % MXU_N == 0
VMEM_LIMIT = 56 * 1024 * 1024

F32 = jnp.float32
BF16 = jnp.bfloat16


def _sigmoid(x):
    return jax.nn.sigmoid(x)


def _silu(x):
    return x * jax.nn.sigmoid(x)


def _rmsnorm(x, g):
    ms = jnp.mean(x * x, axis=-1, keepdims=True)
    return (x * lax.rsqrt(ms + EPS)) * g


def _inproj_kernel(x_ref, g_ref, w_ref, a_ref, gz_ref, gb_ref, q_ref, kv_ref, *scratch,
                   fuse_norm):
    if fuse_norm:
        (hn_ref,) = scratch

        @pl.when(pl.program_id(1) == 0)
        def _():
            hn_ref[...] = _rmsnorm(x_ref[...], g_ref[...]).astype(BF16)

        hn = hn_ref[...]
    else:
        hn = x_ref[...]

    def part(c):
        return jnp.dot(hn, w_ref[:, c * GW:(c + 1) * GW], preferred_element_type=F32)

    a = part(0) * _sigmoid(part(1))
    for s in range(GW // LANES):
        a_ref[s] = a[:, s * LANES:(s + 1) * LANES]
    gz_ref[...] = _silu(part(2))
    gb_ref[...] = _silu(part(3))
    q_ref[...] = part(4).astype(BF16)
    kv = jnp.dot(hn, w_ref[:, 5 * GW:], preferred_element_type=F32)
    for s in range(KVW // LANES):
        kv_s = kv[:, s * LANES:(s + 1) * LANES]
        kv_ref[:, 2 * s * LANES:(2 * s + 1) * LANES] = kv_s.astype(BF16)
        kv_ref[:, (2 * s + 1) * LANES:(2 * s + 2) * LANES] = (
            pltpu.roll(kv_s, HEAD_DIM, 1).astype(BF16))


def _inproj(x, g, w, *, fuse_norm, tm=512):
    T = x.shape[0]
    scratch = [pltpu.VMEM((tm, D_MODEL), BF16)] if fuse_norm else []

    def col(dtype, width=GW):
        return (pl.BlockSpec((tm, width), lambda i, j: (i, j)),
                jax.ShapeDtypeStruct((T, N_GROUP * width), dtype))

    specs, shapes = zip(col(F32), col(F32), col(BF16), col(BF16, 2 * KVW))
    spg = GW // LANES
    return pl.pallas_call(
        functools.partial(_inproj_kernel, fuse_norm=fuse_norm),
        grid=(T // tm, N_GROUP),
        in_specs=[
            pl.BlockSpec((tm, D_MODEL), lambda i, j: (i, 0)),
            pl.BlockSpec((1, D_MODEL), lambda i, j: (0, 0)),
            pl.BlockSpec((D_MODEL, W_GROUP), lambda i, j: (0, j)),
        ],
        out_specs=[pl.BlockSpec((spg, tm, LANES), lambda i, j: (j, i, 0))] + list(specs),
        out_shape=[jax.ShapeDtypeStruct((N_SLAB, T, LANES), F32)] + list(shapes),
        scratch_shapes=scratch,
        compiler_params=pltpu.CompilerParams(
            dimension_semantics=("arbitrary", "arbitrary"),
            vmem_limit_bytes=VMEM_LIMIT),
        name="inproj_norm" if fuse_norm else "inproj",
    )(x, g, w)


def _conv_kernel(ap_ref, ac_ref, an_ref, gz_ref, cw_ref, cb_ref, lg_ref, lb_ref,
                 o_ref, a_ext, y_buf, *, ts, rc, rn):
    i = pl.program_id(1)
    n = pl.num_programs(1)
    zero = jnp.zeros((N_SLAB, HALO, LANES), F32)
    a_ext[:, 0:HALO, :] = jnp.where(i > 0, ap_ref[:, 0], zero)
    a_ext[:, HALO:HALO + ts, :] = ac_ref[:, 0]
    a_ext[:, HALO + ts:, :] = jnp.where(i < n - 1, an_ref[:, 0], zero)

    n_sub = rc // SUBLANES
    first = HALO - CONV_PAD
    n_off = CONV_WIDTH + (n_sub - 1) * SUBLANES

    for c in range(N_SLAB):
        taps = [jnp.broadcast_to(cw_ref[c, k:k + 1, :], (SUBLANES, LANES))
                for k in range(CONV_WIDTH)]
        bias = jnp.broadcast_to(cb_ref[c], (SUBLANES, LANES))

        def conv_chunk(r, carry, c=c, taps=taps, bias=bias):
            r0 = pl.multiple_of(r * rc, rc)
            accs = [[bias, None] for _ in range(n_sub)]
            for o in range(n_off):
                win = a_ext[c, pl.ds(r0 + first + o, SUBLANES, stride=1), :]
                for j in range(n_sub):
                    k = o - j * SUBLANES
                    if 0 <= k < CONV_WIDTH:
                        prod = win * taps[k]
                        cur = accs[j][k % 2]
                        accs[j][k % 2] = prod if cur is None else cur + prod
            y_buf[c, pl.ds(r0, rc), :] = jnp.concatenate([e + o_ for e, o_ in accs], axis=0)
            return carry

        lax.fori_loop(0, ts // rc, conv_chunk, 0)

    def chunk(r, carry):
        r0 = pl.multiple_of(r * rn, rn)
        ys = [y_buf[c, pl.ds(r0, rn), :] for c in range(N_SLAB)]
        tot = ys[0]
        for c in range(1, N_SLAB):
            tot = tot + ys[c]
        mu = jnp.sum(tot, axis=-1, keepdims=True) * (1.0 / W_CONV)
        xcs = [y - mu for y in ys]
        sq = xcs[0] * xcs[0]
        for c in range(1, N_SLAB):
            sq = sq + xcs[c] * xcs[c]
        var = jnp.sum(sq, axis=-1, keepdims=True) * (1.0 / W_CONV)
        rs = lax.rsqrt(var + EPS)
        for c in range(N_SLAB):
            cs = slice(c * LANES, (c + 1) * LANES)
            y = (xcs[c] * rs) * lg_ref[:, cs] + lb_ref[:, cs]
            y = _silu(y) * gz_ref[0, pl.ds(r0, rn), cs]
            o_ref[0, pl.ds(r0, rn), cs] = y.astype(BF16)
        return carry

    lax.fori_loop(0, ts // rn, chunk, 0, unroll=4)


def _conv_branch(a4, gz3, cw, cb, lg, lb, *, ts=512, rc=128, rn=32):
    _, B, S, _ = a4.shape
    nh = ts // HALO
    last_h = S // HALO - 1

    def const(shape):
        return pl.BlockSpec(shape, lambda b, i: (0,) * len(shape))

    return pl.pallas_call(
        functools.partial(_conv_kernel, ts=ts, rc=rc, rn=rn),
        grid=(B, S // ts),
        in_specs=[
            pl.BlockSpec((N_SLAB, 1, HALO, LANES),
                         lambda b, i: (0, b, jnp.maximum(i * nh - 1, 0), 0)),
            pl.BlockSpec((N_SLAB, 1, ts, LANES), lambda b, i: (0, b, i, 0)),
            pl.BlockSpec((N_SLAB, 1, HALO, LANES),
                         lambda b, i: (0, b, jnp.minimum((i + 1) * nh, last_h), 0)),
            pl.BlockSpec((1, ts, W_CONV), lambda b, i: (b, i, 0)),
            const((N_SLAB, CONV_WIDTH, LANES)), const((N_SLAB, 1, LANES)),
            const((1, W_CONV)), const((1, W_CONV))],
        out_specs=pl.BlockSpec((1, ts, W_CONV), lambda b, i: (b, i, 0)),
        out_shape=jax.ShapeDtypeStruct((B, S, W_CONV), BF16),
        scratch_shapes=[pltpu.VMEM((N_SLAB, ts + 2 * HALO, LANES), F32),
                        pltpu.VMEM((N_SLAB, ts, LANES), F32)],
        compiler_params=pltpu.CompilerParams(
            dimension_semantics=("arbitrary", "arbitrary"),
            vmem_limit_bytes=VMEM_LIMIT),
        name="conv_branch",
    )(a4, a4, a4, gz3, cw, cb, lg, lb)


def _attn_kernel(sink_ref, q_ref, kvp_ref, kvc_ref, kvn_ref, gb_ref, bias_ref, o_ref):
    nk = 3 * BLK
    lo = lax.broadcasted_iota(jnp.int32, (nk, LANES), 1) < HEAD_DIM
    lo_q = lax.broadcasted_iota(jnp.int32, (2 * BLK, LANES), 1) < HEAD_DIM
    top = lax.broadcasted_iota(jnp.int32, (2 * BLK, 1), 0) < BLK
    zeros = jnp.zeros((nk, LANES), BF16)
    ones_lo = jnp.where(lo, 1.0, 0.0).astype(BF16)
    ones_hi = jnp.where(lo, 0.0, 1.0).astype(BF16)
    sum_cols = jnp.concatenate([ones_lo, ones_hi], axis=0)
    for h in range(N_KV_HEADS):
        c0 = slice(h * HW, h * HW + LANES)
        c1 = slice(h * HW + LANES, (h + 1) * HW)
        kv = jnp.concatenate([kvp_ref[0, :, c0], kvc_ref[0, :, c0], kvn_ref[0, :, c0]], axis=0)
        vk = jnp.concatenate([kvp_ref[0, :, c1], kvc_ref[0, :, c1], kvn_ref[0, :, c1]], axis=0)
        kblk = jnp.concatenate([jnp.where(lo, kv, zeros), jnp.where(lo, zeros, vk)], axis=0)
        vblk = jnp.concatenate([jnp.where(lo, vk, zeros), jnp.where(lo, zeros, kv)], axis=0)
        vblk = jnp.concatenate([vblk, sum_cols], axis=1)
        hs = slice(h * HW, (h + 1) * HW)
        qh = q_ref[0, :, hs]
        lhs = jnp.concatenate([qh[:, :LANES], qh[:, LANES:]], axis=0)
        s = lax.dot_general(lhs, kblk, (((1,), (1,)), ((), ())), preferred_element_type=F32)
        s = s + bias_ref[0, h]
        sa, sb = s[:, :nk], s[:, nk:]
        g0 = h * GQA_GROUP
        sk_a = jnp.where(top, sink_ref[g0], sink_ref[g0 + 2])
        sk_b = jnp.where(top, sink_ref[g0 + 1], sink_ref[g0 + 3])
        ma = jnp.maximum(jnp.max(sa, axis=-1, keepdims=True), sk_a)
        mb = jnp.maximum(jnp.max(sb, axis=-1, keepdims=True), sk_b)
        pe = jnp.concatenate([jnp.exp2(sa - ma), jnp.exp2(sb - mb)], axis=1).astype(BF16)
        o = jnp.dot(pe, vblk, preferred_element_type=F32)
        den = o[:, LANES:] + jnp.where(lo_q, jnp.exp2(sk_a - ma), jnp.exp2(sk_b - mb))
        y = o[:, :LANES] * (1.0 / den)
        yh = jnp.concatenate([y[:BLK], y[BLK:]], axis=1)
        o_ref[0, :, hs] = (yh * gb_ref[0, :, hs]).astype(BF16)


def _attn_branch(sink, q3, kv3, gb3, bias4):
    B, S, _ = q3.shape
    nb = S // BLK
    kvw = kv3.shape[-1]

    def row(d, width=W_ATT):
        return pl.BlockSpec((1, BLK, width), lambda b, j: (b, jnp.clip(j + d, 0, nb - 1), 0))

    def variant(b, j):
        return (jnp.where(j == 0, 0, jnp.where(j == nb - 1, 2, 1)), 0, 0, 0)

    return pl.pallas_call(
        _attn_kernel,
        grid=(B, nb),
        in_specs=[
            pl.BlockSpec(memory_space=pltpu.SMEM),
            row(0), row(-1, kvw), row(0, kvw), row(1, kvw), row(0),
            pl.BlockSpec((1, N_KV_HEADS, 2 * BLK, 6 * BLK), variant),
        ],
        out_specs=row(0),
        out_shape=jax.ShapeDtypeStruct((B, S, W_ATT), BF16),
        compiler_params=pltpu.CompilerParams(
            dimension_semantics=("arbitrary", "arbitrary"),
            vmem_limit_bytes=VMEM_LIMIT),
        name="attn_branch",
    )(sink, q3, kv3, kv3, kv3, gb3, bias4)


def _outproj_kernel(h_ref, ya_ref, yb_ref, p_ref, woa_ref, wob_ref, wpe_ref, peg_ref, wpg_ref,
                    ng_ref, *out_refs, last):
    h1 = h_ref[...] + jnp.dot(ya_ref[...], woa_ref[...], preferred_element_type=F32)
    h1 = h1 + jnp.dot(yb_ref[...], wob_ref[...], preferred_element_type=F32)
    e = jnp.dot(p_ref[...].astype(BF16), wpe_ref[...], preferred_element_type=F32)
    e = _rmsnorm(e, peg_ref[...])
    gate = jnp.dot(h1.astype(BF16), wpg_ref[...], preferred_element_type=F32)
    h2 = h1 + e * _sigmoid(gate)
    if last:
        (o_ref,) = out_refs
        o_ref[...] = _rmsnorm(h2, ng_ref[...])
    else:
        h_out, hn_out = out_refs
        h_out[...] = h2
        hn_out[...] = _rmsnorm(h2, ng_ref[...]).astype(BF16)


def _outproj(h, ya, yb, p, woa, wob, wpe, peg, wpg, ng, *, last, tm=256):
    T = h.shape[0]

    def row(width):
        return pl.BlockSpec((tm, width), lambda i: (i, 0))

    def const(shape):
        return pl.BlockSpec(shape, lambda i: (0, 0), pipeline_mode=pl.Buffered(1))

    if last:
        out_specs = [row(D_MODEL)]
        out_shape = [jax.ShapeDtypeStruct((T, D_MODEL), F32)]
    else:
        out_specs = [row(D_MODEL), row(D_MODEL)]
        out_shape = [jax.ShapeDtypeStruct((T, D_MODEL), F32),
                     jax.ShapeDtypeStruct((T, D_MODEL), BF16)]
    return pl.pallas_call(
        functools.partial(_outproj_kernel, last=last),
        grid=(T // tm,),
        in_specs=[row(D_MODEL), row(W_CONV), row(W_ATT), row(PLE_DIM),
                  const((W_CONV, D_MODEL)), const((W_ATT, D_MODEL)), const((PLE_DIM, D_MODEL)),
                  const((1, D_MODEL)), const((D_MODEL, D_MODEL)), const((1, D_MODEL))],
        out_specs=out_specs,
        out_shape=out_shape,
        compiler_params=pltpu.CompilerParams(
            dimension_semantics=("arbitrary",),
            vmem_limit_bytes=VMEM_LIMIT),
        name="outproj_final" if last else "outproj",
    )(h, ya, yb, p, woa, wob, wpe, peg, wpg, ng)


def _band_buckets():
    q_off = np.arange(BLK)[:, None]
    k_off = np.arange(3 * BLK)[None, :] - BLK
    rel = k_off - q_off
    half = NUM_BUCKETS // 2
    ret = (rel > 0).astype(np.int32) * half
    n = np.abs(rel)
    max_exact = half // 2
    large = max_exact + (np.log(np.maximum(n, 1) / max_exact)
                         / np.log(MAX_DISTANCE / max_exact)
                         * (half - max_exact)).astype(np.int32)
    large = np.minimum(large, half - 1)
    ret = ret + np.where(n < max_exact, n, large)
    return ret.astype(np.int32), (n <= WINDOW)


def _bias_table(rel_bias):
    buckets, band = _band_buckets()
    onehot = np.zeros((BLK * 3 * BLK, NUM_BUCKETS), np.float32)
    onehot[np.arange(onehot.shape[0]), buckets.reshape(-1)] = 1.0
    bias = jnp.dot(jnp.asarray(onehot), rel_bias.astype(F32),
                   precision=lax.Precision.HIGHEST)
    bias = jnp.transpose(bias.reshape(BLK, 3 * BLK, N_Q_HEADS), (2, 0, 1))
    bias = jnp.where(jnp.asarray(band)[None], bias * LOG2E, NEG)
    col = np.arange(3 * BLK)
    first = jnp.where(jnp.asarray(col < BLK)[None, None], NEG, bias)
    last = jnp.where(jnp.asarray(col >= 2 * BLK)[None, None], NEG, bias)
    tab = jnp.stack([first, bias, last])
    tab = tab.reshape(3, N_KV_HEADS, 2, 2, BLK, 3 * BLK)
    tab = jnp.transpose(tab, (0, 1, 2, 4, 3, 5))
    return tab.reshape(3, N_KV_HEADS, 2 * BLK, 6 * BLK)


def _prep_w_in(w_in):
    o = np.cumsum([W_CONV, W_CONV, W_CONV, W_ATT, W_KV, W_KV, W_ATT])
    a_val, a_glu, a_z = w_in[..., :o[0]], w_in[..., o[0]:o[1]], w_in[..., o[1]:o[2]]
    q, k, v, b_z = (w_in[..., o[2]:o[3]], w_in[..., o[3]:o[4]], w_in[..., o[4]:o[5]],
                    w_in[..., o[5]:o[6]])

    def grouped(w):
        return w.reshape(DEPTH, D_MODEL, N_GROUP, -1).astype(BF16)

    def heads(w):
        return w.reshape(DEPTH, D_MODEL, N_KV_HEADS, HEAD_DIM)

    kv = jnp.concatenate([heads(k), heads(v)], axis=-1)
    parts = [grouped(a_val), grouped(a_glu), grouped(a_z), grouped(b_z),
             grouped(q * (HEAD_DIM ** -0.5 * LOG2E)), grouped(kv)]
    return jnp.concatenate(parts, axis=-1).reshape(DEPTH, D_MODEL, N_GROUP * W_GROUP)


def kernel(x, p, norm_g, w_in, conv_w, conv_b, cln_g, cln_b, sink, rel_bias,
           w_out, w_pe, pe_g, w_pg, final_g):
    B, S, _ = x.shape
    T = B * S
    bias4 = _bias_table(rel_bias)
    w_all = _prep_w_in(w_in)
    wo = w_out.astype(BF16)
    wpe = w_pe.astype(BF16)
    wpg = w_pg.astype(BF16)
    cw = jnp.transpose(conv_w.reshape(DEPTH, CONV_WIDTH, N_SLAB, LANES), (0, 2, 1, 3))
    h = x.reshape(T, D_MODEL)
    hn = h
    for i in range(DEPTH):
        a, gz, gb, q, kv = _inproj(hn, norm_g[i].reshape(1, D_MODEL), w_all[i], fuse_norm=(i == 0))
        ya = _conv_branch(a.reshape(N_SLAB, B, S, LANES), gz.reshape(B, S, W_CONV), cw[i],
                          conv_b[i].reshape(N_SLAB, 1, LANES), cln_g[i].reshape(1, W_CONV),
                          cln_b[i].reshape(1, W_CONV))
        yb = _attn_branch(sink[i] * LOG2E, q.reshape(B, S, W_ATT),
                          kv.reshape(B, S, N_KV_HEADS * HW), gb.reshape(B, S, W_ATT), bias4)
        last = i == DEPTH - 1
        ng = final_g if last else norm_g[i + 1]
        outs = _outproj(h, ya.reshape(T, W_CONV), yb.reshape(T, W_ATT), p[i].reshape(T, PLE_DIM),
                        wo[i, :W_CONV], wo[i, W_CONV:], wpe[i], pe_g[i].reshape(1, D_MODEL),
                        wpg[i], ng.reshape(1, D_MODEL), last=last)
        if last:
            (h,) = outs
        else:
            h, hn = outs
    return h.reshape(B, S, D_MODEL)
```

```python
import functools

import numpy as np
import jax
import jax.numpy as jnp
from jax import lax
from jax.experimental import pallas as pl
from jax.experimental.pallas import tpu as pltpu

D_MODEL = 2048
DEPTH = 4
W_CONV = 1024
HEAD_DIM = 64
N_Q_HEADS = 16
N_KV_HEADS = 4
GQA_GROUP = 4
W_ATT = 1024
W_KV = 256
CONV_WIDTH = 31
CONV_PAD = 15
WINDOW = 128
BLK = 128
NUM_BUCKETS = 32
MAX_DISTANCE = 128
PLE_DIM = 256
EPS = 1e-6
NEG = -1e30
LOG2E = float(np.log2(np.e))

LANES = 128
SUBLANES = 8
HALO = 16
N_SLAB = W_CONV // LANES
HW = GQA_GROUP * HEAD_DIM
MXU_N = 256
N_GROUP = 2
GW = W_ATT // N_GROUP
KVW = (2 * W_KV) // N_GROUP
assert KVW % MXU_N == 0 and GW % MXU_N == 0
VMEM_LIMIT = 56 * 1024 * 1024

F32 = jnp.float32
BF16 = jnp.bfloat16


def _sigmoid(x):
    return jax.nn.sigmoid(x)


def _silu(x):
    return x * jax.nn.sigmoid(x)


def _rmsnorm(x, g):
    ms = jnp.mean(x * x, axis=-1, keepdims=True)
    return (x * lax.rsqrt(ms + EPS)) * g


def _inproj_kernel(x_ref, g_ref, wav_ref, wag_ref, waz_ref, wbz_ref, wq_ref, wk_ref, wv_ref,
                   a_ref, gz_ref, gb_ref, q_ref, kv_ref, *scratch, fuse_norm):
    if fuse_norm:
        (hn_ref,) = scratch

        @pl.when(pl.program_id(1) == 0)
        def _():
            hn_ref[...] = _rmsnorm(x_ref[...], g_ref[...]).astype(BF16)

        hn = hn_ref[...]
    else:
        hn = x_ref[...]

    def proj(w):
        return jnp.dot(hn, w, preferred_element_type=F32)

    a = proj(wav_ref[...]) * _sigmoid(proj(wag_ref[...]))
    for s in range(GW // LANES):
        a_ref[s] = a[:, s * LANES:(s + 1) * LANES]
    gz_ref[...] = _silu(proj(waz_ref[...]))
    gb_ref[...] = _silu(proj(wbz_ref[...]))
    q_ref[...] = proj(wq_ref[...]).astype(BF16)
    kv = proj(jnp.concatenate([wk_ref[...], wv_ref[...]], axis=1))
    k2, v2 = kv[:, :LANES], kv[:, LANES:]
    k2r, v2r = pltpu.roll(k2, HEAD_DIM, 1), pltpu.roll(v2, HEAD_DIM, 1)
    lo = lax.broadcasted_iota(jnp.int32, k2.shape, 1) < HEAD_DIM
    pieces = [jnp.where(lo, k2, v2r), jnp.where(lo, v2, k2r),
              jnp.where(lo, k2r, v2), jnp.where(lo, v2r, k2)]
    for s, piece in enumerate(pieces):
        kv_ref[:, s * LANES:(s + 1) * LANES] = piece.astype(BF16)


def _inproj(x, g, w_in, layer, *, fuse_norm, tm=512):
    T = x.shape[0]
    scratch = [pltpu.VMEM((tm, D_MODEL), BF16)] if fuse_norm else []

    def col(dtype, width=GW):
        return (pl.BlockSpec((tm, width), lambda i, j: (i, j)),
                jax.ShapeDtypeStruct((T, N_GROUP * width), dtype))

    def wcols(offset, width):
        base = offset // width
        assert base * width == offset
        return pl.BlockSpec((None, D_MODEL, width), lambda i, j: (layer, 0, base + j))

    o = np.cumsum([0, W_CONV, W_CONV, W_CONV, W_ATT, W_KV, W_KV])
    kw = W_KV // N_GROUP
    w_specs = [wcols(o[0], GW), wcols(o[1], GW), wcols(o[2], GW), wcols(o[6], GW),
               wcols(o[3], GW), wcols(o[4], kw), wcols(o[5], kw)]
    specs, shapes = zip(col(F32), col(F32), col(BF16), col(BF16, 2 * KVW))
    spg = GW // LANES
    return pl.pallas_call(
        functools.partial(_inproj_kernel, fuse_norm=fuse_norm),
        grid=(T // tm, N_GROUP),
        in_specs=[
            pl.BlockSpec((tm, D_MODEL), lambda i, j: (i, 0)),
            pl.BlockSpec((1, D_MODEL), lambda i, j: (0, 0)),
        ] + w_specs,
        out_specs=[pl.BlockSpec((spg, tm, LANES), lambda i, j: (j, i, 0))] + list(specs),
        out_shape=[jax.ShapeDtypeStruct((N_SLAB, T, LANES), F32)] + list(shapes),
        scratch_shapes=scratch,
        compiler_params=pltpu.CompilerParams(
            dimension_semantics=("arbitrary", "arbitrary"),
            vmem_limit_bytes=VMEM_LIMIT),
        name="inproj_norm" if fuse_norm else "inproj",
    )(x, g, *([w_in] * len(w_specs)))


def _conv_kernel(ap_ref, ac_ref, an_ref, gz_ref, cw_ref, cb_ref, lg_ref, lb_ref,
                 o_ref, a_ext, y_buf, *, ts, rc, rn):
    i = pl.program_id(1)
    n = pl.num_programs(1)
    zero = jnp.zeros((N_SLAB, HALO, LANES), F32)
    a_ext[:, 0:HALO, :] = jnp.where(i > 0, ap_ref[:, 0], zero)
    a_ext[:, HALO:HALO + ts, :] = ac_ref[:, 0]
    a_ext[:, HALO + ts:, :] = jnp.where(i < n - 1, an_ref[:, 0], zero)

    n_sub = rc // SUBLANES
    first = HALO - CONV_PAD
    n_off = CONV_WIDTH + (n_sub - 1) * SUBLANES

    for c in range(N_SLAB):
        taps = [jnp.broadcast_to(cw_ref[c, k:k + 1, :], (SUBLANES, LANES))
                for k in range(CONV_WIDTH)]
        bias = jnp.broadcast_to(cb_ref[c], (SUBLANES, LANES))

        def conv_chunk(r, carry, c=c, taps=taps, bias=bias):
            r0 = pl.multiple_of(r * rc, rc)
            accs = [[bias, None] for _ in range(n_sub)]
            for o in range(n_off):
                win = a_ext[c, pl.ds(r0 + first + o, SUBLANES, stride=1), :]
                for j in range(n_sub):
                    k = o - j * SUBLANES
                    if 0 <= k < CONV_WIDTH:
                        prod = win * taps[k]
                        cur = accs[j][k % 2]
                        accs[j][k % 2] = prod if cur is None else cur + prod
            y_buf[c, pl.ds(r0, rc), :] = jnp.concatenate([e + o_ for e, o_ in accs], axis=0)
            return carry

        lax.fori_loop(0, ts // rc, conv_chunk, 0)

    def chunk(r, carry):
        r0 = pl.multiple_of(r * rn, rn)
        ys = [y_buf[c, pl.ds(r0, rn), :] for c in range(N_SLAB)]
        tot = ys[0]
        for c in range(1, N_SLAB):
            tot = tot + ys[c]
        mu = jnp.sum(tot, axis=-1, keepdims=True) * (1.0 / W_CONV)
        xcs = [y - mu for y in ys]
        sq = xcs[0] * xcs[0]
        for c in range(1, N_SLAB):
            sq = sq + xcs[c] * xcs[c]
        var = jnp.sum(sq, axis=-1, keepdims=True) * (1.0 / W_CONV)
        rs = lax.rsqrt(var + EPS)
        for c in range(N_SLAB):
            cs = slice(c * LANES, (c + 1) * LANES)
            y = (xcs[c] * rs) * lg_ref[:, cs] + lb_ref[:, cs]
            y = _silu(y) * gz_ref[0, pl.ds(r0, rn), cs]
            o_ref[0, pl.ds(r0, rn), cs] = y.astype(BF16)
        return carry

    lax.fori_loop(0, ts // rn, chunk, 0, unroll=4)


def _conv_branch(a4, gz3, cw, cb, lg, lb, *, ts=512, rc=128, rn=32):
    _, B, S, _ = a4.shape
    nh = ts // HALO
    last_h = S // HALO - 1

    def const(shape):
        return pl.BlockSpec(shape, lambda b, i: (0,) * len(shape))

    return pl.pallas_call(
        functools.partial(_conv_kernel, ts=ts, rc=rc, rn=rn),
        grid=(B, S // ts),
        in_specs=[
            pl.BlockSpec((N_SLAB, 1, HALO, LANES),
                         lambda b, i: (0, b, jnp.maximum(i * nh - 1, 0), 0)),
            pl.BlockSpec((N_SLAB, 1, ts, LANES), lambda b, i: (0, b, i, 0)),
            pl.BlockSpec((N_SLAB, 1, HALO, LANES),
                         lambda b, i: (0, b, jnp.minimum((i + 1) * nh, last_h), 0)),
            pl.BlockSpec((1, ts, W_CONV), lambda b, i: (b, i, 0)),
            const((N_SLAB, CONV_WIDTH, LANES)), const((N_SLAB, 1, LANES)),
            const((1, W_CONV)), const((1, W_CONV))],
        out_specs=pl.BlockSpec((1, ts, W_CONV), lambda b, i: (b, i, 0)),
        out_shape=jax.ShapeDtypeStruct((B, S, W_CONV), BF16),
        scratch_shapes=[pltpu.VMEM((N_SLAB, ts + 2 * HALO, LANES), F32),
                        pltpu.VMEM((N_SLAB, ts, LANES), F32)],
        compiler_params=pltpu.CompilerParams(
            dimension_semantics=("arbitrary", "arbitrary"),
            vmem_limit_bytes=VMEM_LIMIT),
        name="conv_branch",
    )(a4, a4, a4, gz3, cw, cb, lg, lb)


def _attn_kernel(sink_ref, q_ref, kvp_ref, kvc_ref, kvn_ref, gb_ref, bias_ref, o_ref):
    nk = 3 * BLK
    lo = lax.broadcasted_iota(jnp.int32, (nk, LANES), 1) < HEAD_DIM
    lo_q = lax.broadcasted_iota(jnp.int32, (2 * BLK, LANES), 1) < HEAD_DIM
    top = lax.broadcasted_iota(jnp.int32, (2 * BLK, 1), 0) < BLK
    zeros = jnp.zeros((nk, LANES), BF16)
    ones_lo = jnp.where(lo, 1.0, 0.0).astype(BF16)
    ones_hi = jnp.where(lo, 0.0, 1.0).astype(BF16)
    sum_cols = jnp.concatenate([ones_lo, ones_hi], axis=0)
    for h in range(N_KV_HEADS):
        c0 = slice(h * HW, h * HW + LANES)
        c1 = slice(h * HW + LANES, (h + 1) * HW)
        kv = jnp.concatenate([kvp_ref[0, :, c0], kvc_ref[0, :, c0], kvn_ref[0, :, c0]], axis=0)
        vk = jnp.concatenate([kvp_ref[0, :, c1], kvc_ref[0, :, c1], kvn_ref[0, :, c1]], axis=0)
        kblk = jnp.concatenate([jnp.where(lo, kv, zeros), jnp.where(lo, zeros, vk)], axis=0)
        vblk = jnp.concatenate([jnp.where(lo, vk, zeros), jnp.where(lo, zeros, kv)], axis=0)
        vblk = jnp.concatenate([vblk, sum_cols], axis=1)
        hs = slice(h * HW, (h + 1) * HW)
        qh = q_ref[0, :, hs]
        lhs = jnp.concatenate([qh[:, :LANES], qh[:, LANES:]], axis=0)
        s = lax.dot_general(lhs, kblk, (((1,), (1,)), ((), ())), preferred_element_type=F32)
        s = s + bias_ref[0, h]
        sa, sb = s[:, :nk], s[:, nk:]
        g0 = h * GQA_GROUP
        sk_a = jnp.where(top, sink_ref[g0], sink_ref[g0 + 2])
        sk_b = jnp.where(top, sink_ref[g0 + 1], sink_ref[g0 + 3])
        ma = jnp.maximum(jnp.max(sa, axis=-1, keepdims=True), sk_a)
        mb = jnp.maximum(jnp.max(sb, axis=-1, keepdims=True), sk_b)
        pe = jnp.concatenate([jnp.exp2(sa - ma), jnp.exp2(sb - mb)], axis=1).astype(BF16)
        o = jnp.dot(pe, vblk, preferred_element_type=F32)
        den = o[:, LANES:] + jnp.where(lo_q, jnp.exp2(sk_a - ma), jnp.exp2(sk_b - mb))
        y = o[:, :LANES] * (1.0 / den)
        yh = jnp.concatenate([y[:BLK], y[BLK:]], axis=1)
        o_ref[0, :, hs] = (yh * gb_ref[0, :, hs]).astype(BF16)


def _attn_branch(sink, q3, kv3, gb3, bias4):
    B, S, _ = q3.shape
    nb = S // BLK
    kvw = kv3.shape[-1]

    def row(d, width=W_ATT):
        return pl.BlockSpec((1, BLK, width), lambda b, j: (b, jnp.clip(j + d, 0, nb - 1), 0))

    def variant(b, j):
        return (jnp.where(j == 0, 0, jnp.where(j == nb - 1, 2, 1)), 0, 0, 0)

    return pl.pallas_call(
        _attn_kernel,
        grid=(B, nb),
        in_specs=[
            pl.BlockSpec(memory_space=pltpu.SMEM),
            row(0), row(-1, kvw), row(0, kvw), row(1, kvw), row(0),
            pl.BlockSpec((1, N_KV_HEADS, 2 * BLK, 6 * BLK), variant),
        ],
        out_specs=row(0),
        out_shape=jax.ShapeDtypeStruct((B, S, W_ATT), BF16),
        compiler_params=pltpu.CompilerParams(
            dimension_semantics=("arbitrary", "arbitrary"),
            vmem_limit_bytes=VMEM_LIMIT),
        name="attn_branch",
    )(sink, q3, kv3, kv3, kv3, gb3, bias4)


def _outproj_kernel(h_ref, ya_ref, yb_ref, p_ref, woa_ref, wob_ref, wpe_ref, peg_ref, wpg_ref,
                    ng_ref, *out_refs, last):
    h1 = h_ref[...] + jnp.dot(ya_ref[...], woa_ref[...], preferred_element_type=F32)
    h1 = h1 + jnp.dot(yb_ref[...], wob_ref[...], preferred_element_type=F32)
    e = jnp.dot(p_ref[...].astype(BF16), wpe_ref[...], preferred_element_type=F32)
    e = _rmsnorm(e, peg_ref[...])
    gate = jnp.dot(h1.astype(BF16), wpg_ref[...], preferred_element_type=F32)
    h2 = h1 + e * _sigmoid(gate)
    if last:
        (o_ref,) = out_refs
        o_ref[...] = _rmsnorm(h2, ng_ref[...])
    else:
        h_out, hn_out = out_refs
        h_out[...] = h2
        hn_out[...] = _rmsnorm(h2, ng_ref[...]).astype(BF16)


def _outproj(h, ya, yb, p, wo, wpe, peg, wpg, ng, layer, *, last, tm=256):
    T = h.shape[0]

    def weight(rows, blk=0):
        return pl.BlockSpec((None, rows, D_MODEL), lambda i: (layer, blk, 0),
                            pipeline_mode=pl.Buffered(1))

    def row(width):
        return pl.BlockSpec((tm, width), lambda i: (i, 0))

    def const(shape):
        return pl.BlockSpec(shape, lambda i: (0, 0), pipeline_mode=pl.Buffered(1))

    if last:
        out_specs = [row(D_MODEL)]
        out_shape = [jax.ShapeDtypeStruct((T, D_MODEL), F32)]
    else:
        out_specs = [row(D_MODEL), row(D_MODEL)]
        out_shape = [jax.ShapeDtypeStruct((T, D_MODEL), F32),
                     jax.ShapeDtypeStruct((T, D_MODEL), BF16)]
    return pl.pallas_call(
        functools.partial(_outproj_kernel, last=last),
        grid=(T // tm,),
        in_specs=[row(D_MODEL), row(W_CONV), row(W_ATT), row(PLE_DIM),
                  weight(W_CONV, 0), weight(W_ATT, 1), weight(PLE_DIM),
                  const((1, D_MODEL)), weight(D_MODEL), const((1, D_MODEL))],
        out_specs=out_specs,
        out_shape=out_shape,
        compiler_params=pltpu.CompilerParams(
            dimension_semantics=("arbitrary",),
            vmem_limit_bytes=VMEM_LIMIT),
        name="outproj_final" if last else "outproj",
    )(h, ya, yb, p, wo, wo, wpe, peg, wpg, ng)


def _band_buckets():
    q_off = np.arange(BLK)[:, None]
    k_off = np.arange(3 * BLK)[None, :] - BLK
    rel = k_off - q_off
    half = NUM_BUCKETS // 2
    ret = (rel > 0).astype(np.int32) * half
    n = np.abs(rel)
    max_exact = half // 2
    large = max_exact + (np.log(np.maximum(n, 1) / max_exact)
                         / np.log(MAX_DISTANCE / max_exact)
                         * (half - max_exact)).astype(np.int32)
    large = np.minimum(large, half - 1)
    ret = ret + np.where(n < max_exact, n, large)
    return ret.astype(np.int32), (n <= WINDOW)


def _bias_table(rel_bias):
    buckets, band = _band_buckets()
    onehot = np.zeros((BLK * 3 * BLK, NUM_BUCKETS), np.float32)
    onehot[np.arange(onehot.shape[0]), buckets.reshape(-1)] = 1.0
    bias = jnp.dot(jnp.asarray(onehot), rel_bias.astype(F32),
                   precision=lax.Precision.HIGHEST)
    bias = jnp.transpose(bias.reshape(BLK, 3 * BLK, N_Q_HEADS), (2, 0, 1))
    bias = jnp.where(jnp.asarray(band)[None], bias * LOG2E, NEG)
    col = np.arange(3 * BLK)
    first = jnp.where(jnp.asarray(col < BLK)[None, None], NEG, bias)
    last = jnp.where(jnp.asarray(col >= 2 * BLK)[None, None], NEG, bias)
    tab = jnp.stack([first, bias, last])
    tab = tab.reshape(3, N_KV_HEADS, 2, 2, BLK, 3 * BLK)
    tab = jnp.transpose(tab, (0, 1, 2, 4, 3, 5))
    return tab.reshape(3, N_KV_HEADS, 2 * BLK, 6 * BLK)


def _prep_w_in(w_in):
    q0 = 3 * W_CONV
    col = np.arange(w_in.shape[-1])
    scale = np.where((col >= q0) & (col < q0 + W_ATT), HEAD_DIM ** -0.5 * LOG2E, 1.0)
    return (w_in * jnp.asarray(scale, F32)).astype(BF16)


def kernel(x, p, norm_g, w_in, conv_w, conv_b, cln_g, cln_b, sink, rel_bias,
           w_out, w_pe, pe_g, w_pg, final_g):
    B, S, _ = x.shape
    T = B * S
    bias4 = _bias_table(rel_bias)
    w_all = _prep_w_in(w_in)
    wo = w_out.astype(BF16)
    wpe = w_pe.astype(BF16)
    wpg = w_pg.astype(BF16)
    cw = jnp.transpose(conv_w.reshape(DEPTH, CONV_WIDTH, N_SLAB, LANES), (0, 2, 1, 3))
    h = x.reshape(T, D_MODEL)
    hn = h
    for i in range(DEPTH):
        a, gz, gb, q, kv = _inproj(hn, norm_g[i].reshape(1, D_MODEL), w_all, i,
                                   fuse_norm=(i == 0))
        ya = _conv_branch(a.reshape(N_SLAB, B, S, LANES), gz.reshape(B, S, W_CONV), cw[i],
                          conv_b[i].reshape(N_SLAB, 1, LANES), cln_g[i].reshape(1, W_CONV),
                          cln_b[i].reshape(1, W_CONV))
        yb = _attn_branch(sink[i] * LOG2E, q.reshape(B, S, W_ATT),
                          kv.reshape(B, S, N_KV_HEADS * HW), gb.reshape(B, S, W_ATT), bias4)
        last = i == DEPTH - 1
        ng = final_g if last else norm_g[i + 1]
        outs = _outproj(h, ya.reshape(T, W_CONV), yb.reshape(T, W_ATT), p[i].reshape(T, PLE_DIM),
                        wo, wpe, pe_g[i].reshape(1, D_MODEL), wpg, ng.reshape(1, D_MODEL), i,
                        last=last)
        if last:
            (h,) = outs
        else:
            h, hn = outs
    return h.reshape(B, S, D_MODEL)
```

```python
import functools

import numpy as np
import jax
import jax.numpy as jnp
from jax import lax
from jax.experimental import pallas as pl
from jax.experimental.pallas import tpu as pltpu

D_MODEL = 2048
DEPTH = 4
W_CONV = 1024
HEAD_DIM = 64
N_Q_HEADS = 16
N_KV_HEADS = 4
GQA_GROUP = 4
W_ATT = 1024
W_KV = 256
CONV_WIDTH = 31
CONV_PAD = 15
WINDOW = 128
BLK = 128
NUM_BUCKETS = 32
MAX_DISTANCE = 128
PLE_DIM = 256
EPS = 1e-6
NEG = -1e30
LOG2E = float(np.log2(np.e))

LANES = 128
SUBLANES = 8
HALO = 16
N_SLAB = W_CONV // LANES
HW = GQA_GROUP * HEAD_DIM
MXU_N = 256
N_GROUP = 2
GW = W_ATT // N_GROUP
KVW = (2 * W_KV) // N_GROUP
assert KVW % MXU_N == 0 and GW % MXU_N == 0
VMEM_LIMIT = 56 * 1024 * 1024

F32 = jnp.float32
BF16 = jnp.bfloat16


def _sigmoid(x):
    return jax.nn.sigmoid(x)


def _silu(x):
    return x * jax.nn.sigmoid(x)


def _rmsnorm(x, g):
    ms = jnp.mean(x * x, axis=-1, keepdims=True)
    return (x * lax.rsqrt(ms + EPS)) * g


def _inproj_kernel(x_ref, g_ref, wav_ref, wag_ref, waz_ref, wbz_ref, wq_ref, wk_ref, wv_ref,
                   a_ref, gz_ref, gb_ref, q_ref, kv_ref, *scratch, fuse_norm):
    if fuse_norm:
        (hn_ref,) = scratch

        @pl.when(pl.program_id(1) == 0)
        def _():
            hn_ref[...] = _rmsnorm(x_ref[...], g_ref[...]).astype(BF16)

        hn = hn_ref[...]
    else:
        hn = x_ref[...]

    def proj(w):
        return jnp.dot(hn, w, preferred_element_type=F32)

    a = proj(wav_ref[...]) * _sigmoid(proj(wag_ref[...]))
    for s in range(GW // LANES):
        a_ref[s] = a[:, s * LANES:(s + 1) * LANES]
    gz_ref[...] = _silu(proj(waz_ref[...]))
    gb_ref[...] = _silu(proj(wbz_ref[...]))
    q_ref[...] = proj(wq_ref[...]).astype(BF16)
    kv = proj(jnp.concatenate([wk_ref[...], wv_ref[...]], axis=1))
    k2, v2 = kv[:, :LANES], kv[:, LANES:]
    k2r, v2r = pltpu.roll(k2, HEAD_DIM, 1), pltpu.roll(v2, HEAD_DIM, 1)
    lo = lax.broadcasted_iota(jnp.int32, k2.shape, 1) < HEAD_DIM
    pieces = [jnp.where(lo, k2, v2r), jnp.where(lo, v2, k2r),
              jnp.where(lo, k2r, v2), jnp.where(lo, v2r, k2)]
    for s, piece in enumerate(pieces):
        kv_ref[:, s * LANES:(s + 1) * LANES] = piece.astype(BF16)


def _inproj(x, g, w_in, layer, *, fuse_norm, tm=512):
    T = x.shape[0]
    scratch = [pltpu.VMEM((tm, D_MODEL), BF16)] if fuse_norm else []

    def col(dtype, width=GW):
        return (pl.BlockSpec((tm, width), lambda i, j: (i, j)),
                jax.ShapeDtypeStruct((T, N_GROUP * width), dtype))

    def wcols(offset, width):
        base = offset // width
        assert base * width == offset
        return pl.BlockSpec((None, D_MODEL, width), lambda i, j: (layer, 0, base + j))

    o = np.cumsum([0, W_CONV, W_CONV, W_CONV, W_ATT, W_KV, W_KV])
    kw = W_KV // N_GROUP
    w_specs = [wcols(o[0], GW), wcols(o[1], GW), wcols(o[2], GW), wcols(o[6], GW),
               wcols(o[3], GW), wcols(o[4], kw), wcols(o[5], kw)]
    specs, shapes = zip(col(F32), col(F32), col(BF16), col(BF16, 2 * KVW))
    spg = GW // LANES
    return pl.pallas_call(
        functools.partial(_inproj_kernel, fuse_norm=fuse_norm),
        grid=(T // tm, N_GROUP),
        in_specs=[
            pl.BlockSpec((tm, D_MODEL), lambda i, j: (i, 0)),
            pl.BlockSpec((1, D_MODEL), lambda i, j: (0, 0)),
        ] + w_specs,
        out_specs=[pl.BlockSpec((spg, tm, LANES), lambda i, j: (j, i, 0))] + list(specs),
        out_shape=[jax.ShapeDtypeStruct((N_SLAB, T, LANES), F32)] + list(shapes),
        scratch_shapes=scratch,
        compiler_params=pltpu.CompilerParams(
            dimension_semantics=("arbitrary", "arbitrary"),
            vmem_limit_bytes=VMEM_LIMIT),
        name="inproj_norm" if fuse_norm else "inproj",
    )(x, g, *([w_in] * len(w_specs)))


def _conv_kernel(ap_ref, ac_ref, an_ref, cw_ref, cb_ref, o_ref, a_ext, *, ts, rc):
    i = pl.program_id(1)
    n = pl.num_programs(1)
    zero = jnp.zeros((N_SLAB, HALO, LANES), F32)
    a_ext[:, 0:HALO, :] = jnp.where(i > 0, ap_ref[:, 0], zero)
    a_ext[:, HALO:HALO + ts, :] = ac_ref[:, 0]
    a_ext[:, HALO + ts:, :] = jnp.where(i < n - 1, an_ref[:, 0], zero)

    n_sub = rc // SUBLANES
    first = HALO - CONV_PAD
    n_off = CONV_WIDTH + (n_sub - 1) * SUBLANES

    for c in range(N_SLAB):
        taps = [jnp.broadcast_to(cw_ref[c, k:k + 1, :], (SUBLANES, LANES))
                for k in range(CONV_WIDTH)]
        bias = jnp.broadcast_to(cb_ref[c], (SUBLANES, LANES))

        def conv_chunk(r, carry, c=c, taps=taps, bias=bias):
            r0 = pl.multiple_of(r * rc, rc)
            accs = [[bias, None] for _ in range(n_sub)]
            for o in range(n_off):
                win = a_ext[c, pl.ds(r0 + first + o, SUBLANES, stride=1), :]
                for j in range(n_sub):
                    k = o - j * SUBLANES
                    if 0 <= k < CONV_WIDTH:
                        prod = win * taps[k]
                        cur = accs[j][k % 2]
                        accs[j][k % 2] = prod if cur is None else cur + prod
            o_ref[c, 0, pl.ds(r0, rc), :] = jnp.concatenate([e + o_ for e, o_ in accs], axis=0)
            return carry

        lax.fori_loop(0, ts // rc, conv_chunk, 0)


def _conv_branch(a4, cw, cb, *, ts=512, rc=128):
    _, B, S, _ = a4.shape
    nh = ts // HALO
    last_h = S // HALO - 1

    def const(shape):
        return pl.BlockSpec(shape, lambda b, i: (0,) * len(shape))

    return pl.pallas_call(
        functools.partial(_conv_kernel, ts=ts, rc=rc),
        grid=(B, S // ts),
        in_specs=[
            pl.BlockSpec((N_SLAB, 1, HALO, LANES),
                         lambda b, i: (0, b, jnp.maximum(i * nh - 1, 0), 0)),
            pl.BlockSpec((N_SLAB, 1, ts, LANES), lambda b, i: (0, b, i, 0)),
            pl.BlockSpec((N_SLAB, 1, HALO, LANES),
                         lambda b, i: (0, b, jnp.minimum((i + 1) * nh, last_h), 0)),
            const((N_SLAB, CONV_WIDTH, LANES)), const((N_SLAB, 1, LANES))],
        out_specs=pl.BlockSpec((N_SLAB, 1, ts, LANES), lambda b, i: (0, b, i, 0)),
        out_shape=jax.ShapeDtypeStruct((N_SLAB, B, S, LANES), F32),
        scratch_shapes=[pltpu.VMEM((N_SLAB, ts + 2 * HALO, LANES), F32)],
        compiler_params=pltpu.CompilerParams(
            dimension_semantics=("arbitrary", "arbitrary"),
            vmem_limit_bytes=VMEM_LIMIT),
        name="conv_branch",
    )(a4, a4, a4, cw, cb)


def _attn_kernel(sink_ref, q_ref, kvp_ref, kvc_ref, kvn_ref, gb_ref, bias_ref, o_ref):
    nk = 3 * BLK
    lo = lax.broadcasted_iota(jnp.int32, (nk, LANES), 1) < HEAD_DIM
    lo_q = lax.broadcasted_iota(jnp.int32, (2 * BLK, LANES), 1) < HEAD_DIM
    top = lax.broadcasted_iota(jnp.int32, (2 * BLK, 1), 0) < BLK
    zeros = jnp.zeros((nk, LANES), BF16)
    ones_lo = jnp.where(lo, 1.0, 0.0).astype(BF16)
    ones_hi = jnp.where(lo, 0.0, 1.0).astype(BF16)
    sum_cols = jnp.concatenate([ones_lo, ones_hi], axis=0)
    for h in range(N_KV_HEADS):
        c0 = slice(h * HW, h * HW + LANES)
        c1 = slice(h * HW + LANES, (h + 1) * HW)
        kv = jnp.concatenate([kvp_ref[0, :, c0], kvc_ref[0, :, c0], kvn_ref[0, :, c0]], axis=0)
        vk = jnp.concatenate([kvp_ref[0, :, c1], kvc_ref[0, :, c1], kvn_ref[0, :, c1]], axis=0)
        kblk = jnp.concatenate([jnp.where(lo, kv, zeros), jnp.where(lo, zeros, vk)], axis=0)
        vblk = jnp.concatenate([jnp.where(lo, vk, zeros), jnp.where(lo, zeros, kv)], axis=0)
        vblk = jnp.concatenate([vblk, sum_cols], axis=1)
        hs = slice(h * HW, (h + 1) * HW)
        qh = q_ref[0, :, hs]
        lhs = jnp.concatenate([qh[:, :LANES], qh[:, LANES:]], axis=0)
        s = lax.dot_general(lhs, kblk, (((1,), (1,)), ((), ())), preferred_element_type=F32)
        s = s + bias_ref[0, h]
        sa, sb = s[:, :nk], s[:, nk:]
        g0 = h * GQA_GROUP
        sk_a = jnp.where(top, sink_ref[g0], sink_ref[g0 + 2])
        sk_b = jnp.where(top, sink_ref[g0 + 1], sink_ref[g0 + 3])
        ma = jnp.maximum(jnp.max(sa, axis=-1, keepdims=True), sk_a)
        mb = jnp.maximum(jnp.max(sb, axis=-1, keepdims=True), sk_b)
        pe = jnp.concatenate([jnp.exp2(sa - ma), jnp.exp2(sb - mb)], axis=1).astype(BF16)
        o = jnp.dot(pe, vblk, preferred_element_type=F32)
        den = o[:, LANES:] + jnp.where(lo_q, jnp.exp2(sk_a - ma), jnp.exp2(sk_b - mb))
        y = o[:, :LANES] * (1.0 / den)
        yh = jnp.concatenate([y[:BLK], y[BLK:]], axis=1)
        o_ref[0, :, hs] = (yh * gb_ref[0, :, hs]).astype(BF16)


def _attn_branch(sink, q3, kv3, gb3, bias4):
    B, S, _ = q3.shape
    nb = S // BLK
    kvw = kv3.shape[-1]

    def row(d, width=W_ATT):
        return pl.BlockSpec((1, BLK, width), lambda b, j: (b, jnp.clip(j + d, 0, nb - 1), 0))

    def variant(b, j):
        return (jnp.where(j == 0, 0, jnp.where(j == nb - 1, 2, 1)), 0, 0, 0)

    return pl.pallas_call(
        _attn_kernel,
        grid=(B, nb),
        in_specs=[
            pl.BlockSpec(memory_space=pltpu.SMEM),
            row(0), row(-1, kvw), row(0, kvw), row(1, kvw), row(0),
            pl.BlockSpec((1, N_KV_HEADS, 2 * BLK, 6 * BLK), variant),
        ],
        out_specs=row(0),
        out_shape=jax.ShapeDtypeStruct((B, S, W_ATT), BF16),
        compiler_params=pltpu.CompilerParams(
            dimension_semantics=("arbitrary", "arbitrary"),
            vmem_limit_bytes=VMEM_LIMIT),
        name="attn_branch",
    )(sink, q3, kv3, kv3, kv3, gb3, bias4)


def _outproj_kernel(h_ref, yc_ref, gz_ref, yb_ref, p_ref, lg_ref, lb_ref, woa_ref, wob_ref,
                    wpe_ref, peg_ref, wpg_ref, ng_ref, *out_refs, last):
    ys = [yc_ref[c] for c in range(N_SLAB)]
    tot = ys[0]
    for c in range(1, N_SLAB):
        tot = tot + ys[c]
    mu = jnp.sum(tot, axis=-1, keepdims=True) * (1.0 / W_CONV)
    xcs = [y - mu for y in ys]
    sq = xcs[0] * xcs[0]
    for c in range(1, N_SLAB):
        sq = sq + xcs[c] * xcs[c]
    var = jnp.sum(sq, axis=-1, keepdims=True) * (1.0 / W_CONV)
    rs = lax.rsqrt(var + EPS)
    ya = []
    for c in range(N_SLAB):
        cs = slice(c * LANES, (c + 1) * LANES)
        y = (xcs[c] * rs) * lg_ref[:, cs] + lb_ref[:, cs]
        ya.append((_silu(y) * gz_ref[:, cs]).astype(BF16))
    ya = jnp.concatenate(ya, axis=1)

    h1 = h_ref[...] + jnp.dot(yb_ref[...], wob_ref[...], preferred_element_type=F32)
    h1 = h1 + jnp.dot(ya, woa_ref[...], preferred_element_type=F32)
    e = jnp.dot(p_ref[...].astype(BF16), wpe_ref[...], preferred_element_type=F32)
    e = _rmsnorm(e, peg_ref[...])
    gate = jnp.dot(h1.astype(BF16), wpg_ref[...], preferred_element_type=F32)
    h2 = h1 + e * _sigmoid(gate)
    if last:
        (o_ref,) = out_refs
        o_ref[...] = _rmsnorm(h2, ng_ref[...])
    else:
        h_out, hn_out = out_refs
        h_out[...] = h2
        hn_out[...] = _rmsnorm(h2, ng_ref[...]).astype(BF16)


def _outproj(h, yc, gz, yb, p, lg, lb, wo, wpe, peg, wpg, ng, layer, *, last, tm=256):
    T = h.shape[0]

    def weight(rows, blk=0):
        return pl.BlockSpec((None, rows, D_MODEL), lambda i: (layer, blk, 0),
                            pipeline_mode=pl.Buffered(1))

    def row(width):
        return pl.BlockSpec((tm, width), lambda i: (i, 0))

    def const(shape):
        return pl.BlockSpec(shape, lambda i: (0, 0), pipeline_mode=pl.Buffered(1))

    if last:
        out_specs = [row(D_MODEL)]
        out_shape = [jax.ShapeDtypeStruct((T, D_MODEL), F32)]
    else:
        out_specs = [row(D_MODEL), row(D_MODEL)]
        out_shape = [jax.ShapeDtypeStruct((T, D_MODEL), F32),
                     jax.ShapeDtypeStruct((T, D_MODEL), BF16)]
    return pl.pallas_call(
        functools.partial(_outproj_kernel, last=last),
        grid=(T // tm,),
        in_specs=[row(D_MODEL),
                  pl.BlockSpec((N_SLAB, tm, LANES), lambda i: (0, i, 0)),
                  row(W_CONV), row(W_ATT),
                  pl.BlockSpec((None, tm, PLE_DIM), lambda i: (layer, i, 0)),
                  const((1, W_CONV)), const((1, W_CONV)),
                  weight(W_CONV, 0), weight(W_ATT, 1), weight(PLE_DIM),
                  const((1, D_MODEL)), weight(D_MODEL), const((1, D_MODEL))],
        out_specs=out_specs,
        out_shape=out_shape,
        compiler_params=pltpu.CompilerParams(
            dimension_semantics=("arbitrary",),
            vmem_limit_bytes=VMEM_LIMIT),
        name="outproj_final" if last else "outproj",
    )(h, yc, gz, yb, p, lg, lb, wo, wo, wpe, peg, wpg, ng)


def _band_buckets():
    q_off = np.arange(BLK)[:, None]
    k_off = np.arange(3 * BLK)[None, :] - BLK
    rel = k_off - q_off
    half = NUM_BUCKETS // 2
    ret = (rel > 0).astype(np.int32) * half
    n = np.abs(rel)
    max_exact = half // 2
    large = max_exact + (np.log(np.maximum(n, 1) / max_exact)
                         / np.log(MAX_DISTANCE / max_exact)
                         * (half - max_exact)).astype(np.int32)
    large = np.minimum(large, half - 1)
    ret = ret + np.where(n < max_exact, n, large)
    return ret.astype(np.int32), (n <= WINDOW)


def _bias_table(rel_bias):
    buckets, band = _band_buckets()
    onehot = np.zeros((BLK * 3 * BLK, NUM_BUCKETS), np.float32)
    onehot[np.arange(onehot.shape[0]), buckets.reshape(-1)] = 1.0
    bias = jnp.dot(rel_bias.astype(F32).T, jnp.asarray(onehot.T),
                   precision=lax.Precision.HIGHEST)
    bias = bias.reshape(N_Q_HEADS, BLK, 3 * BLK)
    bias = jnp.where(jnp.asarray(band)[None], bias * LOG2E, NEG)
    col = np.arange(3 * BLK)
    first = jnp.where(jnp.asarray(col < BLK)[None, None], NEG, bias)
    last = jnp.where(jnp.asarray(col >= 2 * BLK)[None, None], NEG, bias)
    tab = jnp.stack([first, bias, last])
    tab = tab.reshape(3, N_Q_HEADS // 2, 2, BLK, 3 * BLK)
    tab = jnp.concatenate([tab[:, :, 0], tab[:, :, 1]], axis=-1)
    return tab.reshape(3, N_KV_HEADS, 2 * BLK, 6 * BLK)


def _prep_w_in(w_in):
    q0 = 3 * W_CONV
    col = np.arange(w_in.shape[-1])
    scale = np.where((col >= q0) & (col < q0 + W_ATT), HEAD_DIM ** -0.5 * LOG2E, 1.0)
    return (w_in * jnp.asarray(scale, F32)).astype(BF16)


def kernel(x, p, norm_g, w_in, conv_w, conv_b, cln_g, cln_b, sink, rel_bias,
           w_out, w_pe, pe_g, w_pg, final_g):
    B, S, _ = x.shape
    T = B * S
    bias4 = _bias_table(rel_bias)
    w_all = _prep_w_in(w_in)
    wo = w_out.astype(BF16)
    wpe = w_pe.astype(BF16)
    wpg = w_pg.astype(BF16)
    cw = jnp.transpose(conv_w.reshape(DEPTH, CONV_WIDTH, N_SLAB, LANES), (0, 2, 1, 3))
    p3 = p.reshape(DEPTH, T, PLE_DIM)
    h = x.reshape(T, D_MODEL)
    hn = h
    for i in range(DEPTH):
        a, gz, gb, q, kv = _inproj(hn, norm_g[i].reshape(1, D_MODEL), w_all, i,
                                   fuse_norm=(i == 0))
        yc = _conv_branch(a.reshape(N_SLAB, B, S, LANES), cw[i],
                          conv_b[i].reshape(N_SLAB, 1, LANES))
        yb = _attn_branch(sink[i] * LOG2E, q.reshape(B, S, W_ATT),
                          kv.reshape(B, S, N_KV_HEADS * HW), gb.reshape(B, S, W_ATT), bias4)
        last = i == DEPTH - 1
        ng = final_g if last else norm_g[i + 1]
        outs = _outproj(h, yc.reshape(N_SLAB, T, LANES), gz, yb.reshape(T, W_ATT), p3,
                        cln_g[i].reshape(1, W_CONV), cln_b[i].reshape(1, W_CONV),
                        wo, wpe, pe_g[i].reshape(1, D_MODEL), wpg, ng.reshape(1, D_MODEL), i,
                        last=last)
        if last:
            (h,) = outs
        else:
            h, hn = outs
    return h.reshape(B, S, D_MODEL)
```

```python
import functools

import numpy as np
import jax
import jax.numpy as jnp
from jax import lax
from jax.experimental import pallas as pl
from jax.experimental.pallas import tpu as pltpu

D_MODEL = 2048
DEPTH = 4
W_CONV = 1024
HEAD_DIM = 64
N_Q_HEADS = 16
N_KV_HEADS = 4
GQA_GROUP = 4
W_ATT = 1024
W_KV = 256
CONV_WIDTH = 31
CONV_PAD = 15
WINDOW = 128
BLK = 128
NUM_BUCKETS = 32
MAX_DISTANCE = 128
PLE_DIM = 256
EPS = 1e-6
NEG = -1e30
LOG2E = float(np.log2(np.e))

LANES = 128
SUBLANES = 8
HALO = 16
N_SLAB = W_CONV // LANES
HW = GQA_GROUP * HEAD_DIM
MXU_N = 256
N_GROUP = 2
GW = W_ATT // N_GROUP
KVW = (2 * W_KV) // N_GROUP
assert KVW % MXU_N == 0 and GW % MXU_N == 0
VMEM_LIMIT = 56 * 1024 * 1024

F32 = jnp.float32
BF16 = jnp.bfloat16


def _sigmoid(x):
    return jax.nn.sigmoid(x)


def _silu(x):
    return x * jax.nn.sigmoid(x)


def _rmsnorm(x, g):
    ms = jnp.mean(x * x, axis=-1, keepdims=True)
    return (x * lax.rsqrt(ms + EPS)) * g


def _inproj_kernel(x_ref, g_ref, wav_ref, wag_ref, waz_ref, wbz_ref, wq_ref, wk_ref, wv_ref,
                   a_ref, gz_ref, gb_ref, q_ref, kv_ref, *scratch, fuse_norm):
    if fuse_norm:
        (hn_ref,) = scratch

        @pl.when(pl.program_id(1) == 0)
        def _():
            hn_ref[...] = _rmsnorm(x_ref[...], g_ref[...]).astype(BF16)

        hn = hn_ref[...]
    else:
        hn = x_ref[...]

    def proj(w):
        return jnp.dot(hn, w, preferred_element_type=F32)

    a = proj(wav_ref[...]) * _sigmoid(proj(wag_ref[...]))
    for s in range(GW // LANES):
        a_ref[s] = a[:, s * LANES:(s + 1) * LANES]
    gz_ref[...] = _silu(proj(waz_ref[...]))
    gb_ref[...] = _silu(proj(wbz_ref[...]))
    q_ref[...] = proj(wq_ref[...]).astype(BF16)
    kv = proj(jnp.concatenate([wk_ref[...], wv_ref[...]], axis=1))
    k2, v2 = kv[:, :LANES], kv[:, LANES:]
    k2r, v2r = pltpu.roll(k2, HEAD_DIM, 1), pltpu.roll(v2, HEAD_DIM, 1)
    lo = lax.broadcasted_iota(jnp.int32, k2.shape, 1) < HEAD_DIM
    pieces = [jnp.where(lo, k2, v2r), jnp.where(lo, v2, k2r),
              jnp.where(lo, k2r, v2), jnp.where(lo, v2r, k2)]
    for s, piece in enumerate(pieces):
        kv_ref[:, s * LANES:(s + 1) * LANES] = piece.astype(BF16)


def _inproj(x, g, w_in, layer, *, fuse_norm, tm=512):
    T = x.shape[0]
    scratch = [pltpu.VMEM((tm, D_MODEL), BF16)] if fuse_norm else []

    def col(dtype, width=GW):
        return (pl.BlockSpec((tm, width), lambda i, j: (i, j)),
                jax.ShapeDtypeStruct((T, N_GROUP * width), dtype))

    def wcols(offset, width):
        base = offset // width
        assert base * width == offset
        return pl.BlockSpec((None, D_MODEL, width), lambda i, j: (layer, 0, base + j))

    o = np.cumsum([0, W_CONV, W_CONV, W_CONV, W_ATT, W_KV, W_KV])
    kw = W_KV // N_GROUP
    w_specs = [wcols(o[0], GW), wcols(o[1], GW), wcols(o[2], GW), wcols(o[6], GW),
               wcols(o[3], GW), wcols(o[4], kw), wcols(o[5], kw)]
    specs, shapes = zip(col(F32), col(F32), col(BF16), col(BF16, 2 * KVW))
    spg = GW // LANES
    return pl.pallas_call(
        functools.partial(_inproj_kernel, fuse_norm=fuse_norm),
        grid=(T // tm, N_GROUP),
        in_specs=[
            pl.BlockSpec((tm, D_MODEL), lambda i, j: (i, 0)),
            pl.BlockSpec((1, D_MODEL), lambda i, j: (0, 0)),
        ] + w_specs,
        out_specs=[pl.BlockSpec((spg, tm, LANES), lambda i, j: (j, i, 0))] + list(specs),
        out_shape=[jax.ShapeDtypeStruct((N_SLAB, T, LANES), F32)] + list(shapes),
        scratch_shapes=scratch,
        compiler_params=pltpu.CompilerParams(
            dimension_semantics=("arbitrary", "arbitrary"),
            vmem_limit_bytes=VMEM_LIMIT),
        name="inproj_norm" if fuse_norm else "inproj",
    )(x, g, *([w_in] * len(w_specs)))


def _conv_kernel(ap_ref, ac_ref, an_ref, cw_ref, cb_ref, o_ref, a_ext, *, ts, rc):
    i = pl.program_id(1)
    n = pl.num_programs(1)
    zero = jnp.zeros((N_SLAB, HALO, LANES), F32)
    a_ext[:, 0:HALO, :] = jnp.where(i > 0, ap_ref[:, 0], zero)
    a_ext[:, HALO:HALO + ts, :] = ac_ref[:, 0]
    a_ext[:, HALO + ts:, :] = jnp.where(i < n - 1, an_ref[:, 0], zero)

    n_sub = rc // SUBLANES
    first = HALO - CONV_PAD
    n_off = CONV_WIDTH + (n_sub - 1) * SUBLANES

    for c in range(N_SLAB):
        taps = [jnp.broadcast_to(cw_ref[c, k:k + 1, :], (SUBLANES, LANES))
                for k in range(CONV_WIDTH)]
        bias = jnp.broadcast_to(cb_ref[c], (SUBLANES, LANES))

        def conv_chunk(r, carry, c=c, taps=taps, bias=bias):
            r0 = pl.multiple_of(r * rc, rc)
            accs = [[bias, None] for _ in range(n_sub)]
            for o in range(n_off):
                win = a_ext[c, pl.ds(r0 + first + o, SUBLANES, stride=1), :]
                for j in range(n_sub):
                    k = o - j * SUBLANES
                    if 0 <= k < CONV_WIDTH:
                        prod = win * taps[k]
                        cur = accs[j][k % 2]
                        accs[j][k % 2] = prod if cur is None else cur + prod
            o_ref[c, 0, pl.ds(r0, rc), :] = jnp.concatenate([e + o_ for e, o_ in accs], axis=0)
            return carry

        lax.fori_loop(0, ts // rc, conv_chunk, 0)


def _conv_branch(a4, cw, cb, *, ts=512, rc=128):
    _, B, S, _ = a4.shape
    nh = ts // HALO
    last_h = S // HALO - 1

    def const(shape):
        return pl.BlockSpec(shape, lambda b, i: (0,) * len(shape))

    return pl.pallas_call(
        functools.partial(_conv_kernel, ts=ts, rc=rc),
        grid=(B, S // ts),
        in_specs=[
            pl.BlockSpec((N_SLAB, 1, HALO, LANES),
                         lambda b, i: (0, b, jnp.maximum(i * nh - 1, 0), 0)),
            pl.BlockSpec((N_SLAB, 1, ts, LANES), lambda b, i: (0, b, i, 0)),
            pl.BlockSpec((N_SLAB, 1, HALO, LANES),
                         lambda b, i: (0, b, jnp.minimum((i + 1) * nh, last_h), 0)),
            const((N_SLAB, CONV_WIDTH, LANES)), const((N_SLAB, 1, LANES))],
        out_specs=pl.BlockSpec((N_SLAB, 1, ts, LANES), lambda b, i: (0, b, i, 0)),
        out_shape=jax.ShapeDtypeStruct((N_SLAB, B, S, LANES), F32),
        scratch_shapes=[pltpu.VMEM((N_SLAB, ts + 2 * HALO, LANES), F32)],
        compiler_params=pltpu.CompilerParams(
            dimension_semantics=("arbitrary", "arbitrary"),
            vmem_limit_bytes=VMEM_LIMIT),
        name="conv_branch",
    )(a4, a4, a4, cw, cb)


def _attn_kernel(sink_ref, q_ref, kvp_ref, kvc_ref, kvn_ref, gb_ref, bias_ref, o_ref, *, qb):
    step = pl.program_id(1)
    last_blk = pl.num_programs(1) * qb - 1
    nk = 3 * BLK
    lo = lax.broadcasted_iota(jnp.int32, (nk, LANES), 1) < HEAD_DIM
    lo_q = lax.broadcasted_iota(jnp.int32, (2 * BLK, LANES), 1) < HEAD_DIM
    top = lax.broadcasted_iota(jnp.int32, (2 * BLK, 1), 0) < BLK
    zeros = jnp.zeros((nk, LANES), BF16)
    ones_lo = jnp.where(lo, 1.0, 0.0).astype(BF16)
    ones_hi = jnp.where(lo, 0.0, 1.0).astype(BF16)
    sum_cols = jnp.concatenate([ones_lo, ones_hi], axis=0)

    def keys(sub, cols):
        parts = []
        for t in (sub - 1, sub, sub + 1):
            if t < 0:
                parts.append(kvp_ref[0, :, cols])
            elif t >= qb:
                parts.append(kvn_ref[0, :, cols])
            else:
                parts.append(kvc_ref[0, t * BLK:(t + 1) * BLK, cols])
        return jnp.concatenate(parts, axis=0)

    for sub, h in [(sub, h) for sub in range(qb) for h in range(N_KV_HEADS)]:
        blk = step * qb + sub
        variant = jnp.where(blk == 0, 0, jnp.where(blk == last_blk, 2, 1))
        rows = slice(sub * BLK, (sub + 1) * BLK)
        kv = keys(sub, slice(h * HW, h * HW + LANES))
        vk = keys(sub, slice(h * HW + LANES, (h + 1) * HW))
        kblk = jnp.concatenate([jnp.where(lo, kv, zeros), jnp.where(lo, zeros, vk)], axis=0)
        vblk = jnp.concatenate([jnp.where(lo, vk, zeros), jnp.where(lo, zeros, kv)], axis=0)
        vblk = jnp.concatenate([vblk, sum_cols], axis=1)
        hs = slice(h * HW, (h + 1) * HW)
        qh = q_ref[0, rows, hs]
        lhs = jnp.concatenate([qh[:, :LANES], qh[:, LANES:]], axis=0)
        s = lax.dot_general(lhs, kblk, (((1,), (1,)), ((), ())), preferred_element_type=F32)
        s = s + bias_ref[variant, h]
        sa, sb = s[:, :nk], s[:, nk:]
        g0 = h * GQA_GROUP
        sk_a = jnp.where(top, sink_ref[g0], sink_ref[g0 + 2])
        sk_b = jnp.where(top, sink_ref[g0 + 1], sink_ref[g0 + 3])
        ma = jnp.maximum(jnp.max(sa, axis=-1, keepdims=True), sk_a)
        mb = jnp.maximum(jnp.max(sb, axis=-1, keepdims=True), sk_b)
        pe = jnp.concatenate([jnp.exp2(sa - ma), jnp.exp2(sb - mb)], axis=1).astype(BF16)
        o = jnp.dot(pe, vblk, preferred_element_type=F32)
        den = o[:, LANES:] + jnp.where(lo_q, jnp.exp2(sk_a - ma), jnp.exp2(sk_b - mb))
        y = o[:, :LANES] * (1.0 / den)
        yh = jnp.concatenate([y[:BLK], y[BLK:]], axis=1)
        o_ref[0, rows, hs] = (yh * gb_ref[0, rows, hs]).astype(BF16)


def _attn_branch(sink, q3, kv3, gb3, bias4, *, qb=4):
    B, S, _ = q3.shape
    nb = S // BLK
    kvw = kv3.shape[-1]

    def row(width=W_ATT):
        return pl.BlockSpec((1, qb * BLK, width), lambda b, j: (b, j, 0))

    def halo(d):
        return pl.BlockSpec((1, BLK, kvw),
                            lambda b, j: (b, jnp.clip(j * qb + d, 0, nb - 1), 0))

    return pl.pallas_call(
        functools.partial(_attn_kernel, qb=qb),
        grid=(B, nb // qb),
        in_specs=[
            pl.BlockSpec(memory_space=pltpu.SMEM),
            row(), halo(-1), row(kvw), halo(qb), row(),
            pl.BlockSpec(bias4.shape, lambda b, j: (0, 0, 0, 0), pipeline_mode=pl.Buffered(1)),
        ],
        out_specs=row(),
        out_shape=jax.ShapeDtypeStruct((B, S, W_ATT), BF16),
        compiler_params=pltpu.CompilerParams(
            dimension_semantics=("arbitrary", "arbitrary"),
            vmem_limit_bytes=VMEM_LIMIT),
        name="attn_branch",
    )(sink, q3, kv3, kv3, kv3, gb3, bias4)


def _outproj_kernel(h_ref, yc_ref, gz_ref, yb_ref, p_ref, lg_ref, lb_ref, woa_ref, wob_ref,
                    wpe_ref, peg_ref, wpg_ref, ng_ref, *out_refs, last):
    ys = [yc_ref[c] for c in range(N_SLAB)]
    tot = ys[0]
    for c in range(1, N_SLAB):
        tot = tot + ys[c]
    mu = jnp.sum(tot, axis=-1, keepdims=True) * (1.0 / W_CONV)
    xcs = [y - mu for y in ys]
    sq = xcs[0] * xcs[0]
    for c in range(1, N_SLAB):
        sq = sq + xcs[c] * xcs[c]
    var = jnp.sum(sq, axis=-1, keepdims=True) * (1.0 / W_CONV)
    rs = lax.rsqrt(var + EPS)
    ya = []
    for c in range(N_SLAB):
        cs = slice(c * LANES, (c + 1) * LANES)
        y = (xcs[c] * rs) * lg_ref[:, cs] + lb_ref[:, cs]
        ya.append((_silu(y) * gz_ref[:, cs]).astype(BF16))
    ya = jnp.concatenate(ya, axis=1)

    h1 = h_ref[...] + jnp.dot(yb_ref[...], wob_ref[...], preferred_element_type=F32)
    h1 = h1 + jnp.dot(ya, woa_ref[...], preferred_element_type=F32)
    e = jnp.dot(p_ref[...].astype(BF16), wpe_ref[...], preferred_element_type=F32)
    e = _rmsnorm(e, peg_ref[...])
    gate = jnp.dot(h1.astype(BF16), wpg_ref[...], preferred_element_type=F32)
    h2 = h1 + e * _sigmoid(gate)
    if last:
        (o_ref,) = out_refs
        o_ref[...] = _rmsnorm(h2, ng_ref[...])
    else:
        h_out, hn_out = out_refs
        h_out[...] = h2
        hn_out[...] = _rmsnorm(h2, ng_ref[...]).astype(BF16)


def _outproj(h, yc, gz, yb, p, lg, lb, wo, wpe, peg, wpg, ng, layer, *, last, tm=256):
    T = h.shape[0]

    def weight(rows, blk=0):
        return pl.BlockSpec((None, rows, D_MODEL), lambda i: (layer, blk, 0),
                            pipeline_mode=pl.Buffered(1))

    def row(width):
        return pl.BlockSpec((tm, width), lambda i: (i, 0))

    def const(shape):
        return pl.BlockSpec(shape, lambda i: (0, 0), pipeline_mode=pl.Buffered(1))

    if last:
        out_specs = [row(D_MODEL)]
        out_shape = [jax.ShapeDtypeStruct((T, D_MODEL), F32)]
    else:
        out_specs = [row(D_MODEL), row(D_MODEL)]
        out_shape = [jax.ShapeDtypeStruct((T, D_MODEL), F32),
                     jax.ShapeDtypeStruct((T, D_MODEL), BF16)]
    return pl.pallas_call(
        functools.partial(_outproj_kernel, last=last),
        grid=(T // tm,),
        in_specs=[row(D_MODEL),
                  pl.BlockSpec((N_SLAB, tm, LANES), lambda i: (0, i, 0)),
                  row(W_CONV), row(W_ATT),
                  pl.BlockSpec((None, tm, PLE_DIM), lambda i: (layer, i, 0)),
                  const((1, W_CONV)), const((1, W_CONV)),
                  weight(W_CONV, 0), weight(W_ATT, 1), weight(PLE_DIM),
                  const((1, D_MODEL)), weight(D_MODEL), const((1, D_MODEL))],
        out_specs=out_specs,
        out_shape=out_shape,
        compiler_params=pltpu.CompilerParams(
            dimension_semantics=("arbitrary",),
            vmem_limit_bytes=VMEM_LIMIT),
        name="outproj_final" if last else "outproj",
    )(h, yc, gz, yb, p, lg, lb, wo, wo, wpe, peg, wpg, ng)


def _band_buckets():
    q_off = np.arange(BLK)[:, None]
    k_off = np.arange(3 * BLK)[None, :] - BLK
    rel = k_off - q_off
    half = NUM_BUCKETS // 2
    ret = (rel > 0).astype(np.int32) * half
    n = np.abs(rel)
    max_exact = half // 2
    large = max_exact + (np.log(np.maximum(n, 1) / max_exact)
                         / np.log(MAX_DISTANCE / max_exact)
                         * (half - max_exact)).astype(np.int32)
    large = np.minimum(large, half - 1)
    ret = ret + np.where(n < max_exact, n, large)
    return ret.astype(np.int32), (n <= WINDOW)


def _bias_table(rel_bias):
    buckets, band = _band_buckets()
    onehot = np.zeros((BLK * 3 * BLK, NUM_BUCKETS), np.float32)
    onehot[np.arange(onehot.shape[0]), buckets.reshape(-1)] = 1.0
    bias = jnp.dot(rel_bias.astype(F32).T, jnp.asarray(onehot.T),
                   precision=lax.Precision.HIGHEST)
    bias = bias.reshape(N_Q_HEADS, BLK, 3 * BLK)
    bias = jnp.where(jnp.asarray(band)[None], bias * LOG2E, NEG)
    col = np.arange(3 * BLK)
    first = jnp.where(jnp.asarray(col < BLK)[None, None], NEG, bias)
    last = jnp.where(jnp.asarray(col >= 2 * BLK)[None, None], NEG, bias)
    tab = jnp.stack([first, bias, last])
    tab = tab.reshape(3, N_Q_HEADS // 2, 2, BLK, 3 * BLK)
    tab = jnp.concatenate([tab[:, :, 0], tab[:, :, 1]], axis=-1)
    return tab.reshape(3, N_KV_HEADS, 2 * BLK, 6 * BLK)


def _prep_w_in(w_in):
    q0 = 3 * W_CONV
    col = np.arange(w_in.shape[-1])
    scale = np.where((col >= q0) & (col < q0 + W_ATT), HEAD_DIM ** -0.5 * LOG2E, 1.0)
    return (w_in * jnp.asarray(scale, F32)).astype(BF16)


def kernel(x, p, norm_g, w_in, conv_w, conv_b, cln_g, cln_b, sink, rel_bias,
           w_out, w_pe, pe_g, w_pg, final_g):
    B, S, _ = x.shape
    T = B * S
    bias4 = _bias_table(rel_bias)
    w_all = _prep_w_in(w_in)
    wo = w_out.astype(BF16)
    wpe = w_pe.astype(BF16)
    wpg = w_pg.astype(BF16)
    cw = jnp.transpose(conv_w.reshape(DEPTH, CONV_WIDTH, N_SLAB, LANES), (0, 2, 1, 3))
    p3 = p.reshape(DEPTH, T, PLE_DIM)
    h = x.reshape(T, D_MODEL)
    hn = h
    for i in range(DEPTH):
        a, gz, gb, q, kv = _inproj(hn, norm_g[i].reshape(1, D_MODEL), w_all, i,
                                   fuse_norm=(i == 0))
        yc = _conv_branch(a.reshape(N_SLAB, B, S, LANES), cw[i],
                          conv_b[i].reshape(N_SLAB, 1, LANES))
        yb = _attn_branch(sink[i] * LOG2E, q.reshape(B, S, W_ATT),
                          kv.reshape(B, S, N_KV_HEADS * HW), gb.reshape(B, S, W_ATT), bias4)
        last = i == DEPTH - 1
        ng = final_g if last else norm_g[i + 1]
        outs = _outproj(h, yc.reshape(N_SLAB, T, LANES), gz, yb.reshape(T, W_ATT), p3,
                        cln_g[i].reshape(1, W_CONV), cln_b[i].reshape(1, W_CONV),
                        wo, wpe, pe_g[i].reshape(1, D_MODEL), wpg, ng.reshape(1, D_MODEL), i,
                        last=last)
        if last:
            (h,) = outs
        else:
            h, hn = outs
    return h.reshape(B, S, D_MODEL)
```

```python
import functools

import numpy as np
import jax
import jax.numpy as jnp
from jax import lax
from jax.experimental import pallas as pl
from jax.experimental.pallas import tpu as pltpu

D_MODEL = 2048
DEPTH = 4
W_CONV = 1024
HEAD_DIM = 64
N_Q_HEADS = 16
N_KV_HEADS = 4
GQA_GROUP = 4
W_ATT = 1024
W_KV = 256
CONV_WIDTH = 31
CONV_PAD = 15
WINDOW = 128
BLK = 128
NUM_BUCKETS = 32
MAX_DISTANCE = 128
PLE_DIM = 256
EPS = 1e-6
NEG = -1e30
LOG2E = float(np.log2(np.e))

LANES = 128
SUBLANES = 8
HALO = 16
N_SLAB = W_CONV // LANES
HW = GQA_GROUP * HEAD_DIM
MXU_N = 256
N_GROUP = 2
GW = W_ATT // N_GROUP
KVW = (2 * W_KV) // N_GROUP
assert KVW % MXU_N == 0 and GW % MXU_N == 0
VMEM_LIMIT = 56 * 1024 * 1024

F32 = jnp.float32
BF16 = jnp.bfloat16


def _sigmoid(x):
    return jax.nn.sigmoid(x)


def _silu(x):
    return x * jax.nn.sigmoid(x)


def _rmsnorm(x, g):
    ms = jnp.mean(x * x, axis=-1, keepdims=True)
    return (x * lax.rsqrt(ms + EPS)) * g


def _inproj_kernel(x_ref, g_ref, wav_ref, wag_ref, waz_ref, wbz_ref, wq_ref, wk_ref, wv_ref,
                   a_ref, gz_ref, gb_ref, q_ref, kv_ref, *scratch, fuse_norm):
    if fuse_norm:
        (hn_ref,) = scratch

        @pl.when(pl.program_id(1) == 0)
        def _():
            hn_ref[...] = _rmsnorm(x_ref[...], g_ref[...]).astype(BF16)

        hn = hn_ref[...]
    else:
        hn = x_ref[...]

    def proj(w):
        return jnp.dot(hn, w, preferred_element_type=F32)

    a = proj(wav_ref[...]) * _sigmoid(proj(wag_ref[...]))
    for s in range(GW // LANES):
        a_ref[s] = a[:, s * LANES:(s + 1) * LANES]
    gz_ref[...] = _silu(proj(waz_ref[...]))
    gb_ref[...] = _silu(proj(wbz_ref[...]))
    q_ref[...] = proj(wq_ref[...]).astype(BF16)
    kv = proj(jnp.concatenate([wk_ref[...], wv_ref[...]], axis=1))
    k2, v2 = kv[:, :LANES], kv[:, LANES:]
    k2r, v2r = pltpu.roll(k2, HEAD_DIM, 1), pltpu.roll(v2, HEAD_DIM, 1)
    lo = lax.broadcasted_iota(jnp.int32, k2.shape, 1) < HEAD_DIM
    pieces = [jnp.where(lo, k2, v2r), jnp.where(lo, v2, k2r),
              jnp.where(lo, k2r, v2), jnp.where(lo, v2r, k2)]
    for s, piece in enumerate(pieces):
        kv_ref[:, s * LANES:(s + 1) * LANES] = piece.astype(BF16)


def _inproj(x, g, w_in, layer, *, fuse_norm, tm=512):
    T = x.shape[0]
    scratch = [pltpu.VMEM((tm, D_MODEL), BF16)] if fuse_norm else []

    def col(dtype, width=GW):
        return (pl.BlockSpec((tm, width), lambda i, j: (i, j)),
                jax.ShapeDtypeStruct((T, N_GROUP * width), dtype))

    def wcols(offset, width):
        base = offset // width
        assert base * width == offset
        return pl.BlockSpec((None, D_MODEL, width), lambda i, j: (layer, 0, base + j))

    o = np.cumsum([0, W_CONV, W_CONV, W_CONV, W_ATT, W_KV, W_KV])
    kw = W_KV // N_GROUP
    w_specs = [wcols(o[0], GW), wcols(o[1], GW), wcols(o[2], GW), wcols(o[6], GW),
               wcols(o[3], GW), wcols(o[4], kw), wcols(o[5], kw)]
    specs, shapes = zip(col(F32), col(F32), col(BF16), col(BF16, 2 * KVW))
    spg = GW // LANES
    return pl.pallas_call(
        functools.partial(_inproj_kernel, fuse_norm=fuse_norm),
        grid=(T // tm, N_GROUP),
        in_specs=[
            pl.BlockSpec((tm, D_MODEL), lambda i, j: (i, 0)),
            pl.BlockSpec((1, D_MODEL), lambda i, j: (0, 0)),
        ] + w_specs,
        out_specs=[pl.BlockSpec((spg, tm, LANES), lambda i, j: (j, i, 0))] + list(specs),
        out_shape=[jax.ShapeDtypeStruct((N_SLAB, T, LANES), F32)] + list(shapes),
        scratch_shapes=scratch,
        compiler_params=pltpu.CompilerParams(
            dimension_semantics=("arbitrary", "arbitrary"),
            vmem_limit_bytes=VMEM_LIMIT),
        name="inproj_norm" if fuse_norm else "inproj",
    )(x, g, *([w_in] * len(w_specs)))


def _conv_kernel(ap_ref, ac_ref, an_ref, cw_ref, cb_ref, o_ref, a_ext, *, ts, rc):
    i = pl.program_id(1)
    n = pl.num_programs(1)
    zero = jnp.zeros((N_SLAB, HALO, LANES), F32)
    a_ext[:, 0:HALO, :] = jnp.where(i > 0, ap_ref[:, 0], zero)
    a_ext[:, HALO:HALO + ts, :] = ac_ref[:, 0]
    a_ext[:, HALO + ts:, :] = jnp.where(i < n - 1, an_ref[:, 0], zero)

    n_sub = rc // SUBLANES
    first = HALO - CONV_PAD
    n_off = CONV_WIDTH + (n_sub - 1) * SUBLANES

    for c in range(N_SLAB):
        taps = [jnp.broadcast_to(cw_ref[c, k:k + 1, :], (SUBLANES, LANES))
                for k in range(CONV_WIDTH)]
        bias = jnp.broadcast_to(cb_ref[c], (SUBLANES, LANES))

        def conv_chunk(r, carry, c=c, taps=taps, bias=bias):
            r0 = pl.multiple_of(r * rc, rc)
            accs = [[bias, None] for _ in range(n_sub)]
            for o in range(n_off):
                win = a_ext[c, pl.ds(r0 + first + o, SUBLANES, stride=1), :]
                for j in range(n_sub):
                    k = o - j * SUBLANES
                    if 0 <= k < CONV_WIDTH:
                        prod = win * taps[k]
                        cur = accs[j][k % 2]
                        accs[j][k % 2] = prod if cur is None else cur + prod
            o_ref[c, 0, pl.ds(r0, rc), :] = jnp.concatenate([e + o_ for e, o_ in accs], axis=0)
            return carry

        lax.fori_loop(0, ts // rc, conv_chunk, 0)


def _conv_branch(a4, cw, cb, *, ts=1024, rc=128):
    _, B, S, _ = a4.shape
    nh = ts // HALO
    last_h = S // HALO - 1

    def const(shape):
        return pl.BlockSpec(shape, lambda b, i: (0,) * len(shape))

    return pl.pallas_call(
        functools.partial(_conv_kernel, ts=ts, rc=rc),
        grid=(B, S // ts),
        in_specs=[
            pl.BlockSpec((N_SLAB, 1, HALO, LANES),
                         lambda b, i: (0, b, jnp.maximum(i * nh - 1, 0), 0)),
            pl.BlockSpec((N_SLAB, 1, ts, LANES), lambda b, i: (0, b, i, 0)),
            pl.BlockSpec((N_SLAB, 1, HALO, LANES),
                         lambda b, i: (0, b, jnp.minimum((i + 1) * nh, last_h), 0)),
            const((N_SLAB, CONV_WIDTH, LANES)), const((N_SLAB, 1, LANES))],
        out_specs=pl.BlockSpec((N_SLAB, 1, ts, LANES), lambda b, i: (0, b, i, 0)),
        out_shape=jax.ShapeDtypeStruct((N_SLAB, B, S, LANES), F32),
        scratch_shapes=[pltpu.VMEM((N_SLAB, ts + 2 * HALO, LANES), F32)],
        compiler_params=pltpu.CompilerParams(
            dimension_semantics=("arbitrary", "arbitrary"),
            vmem_limit_bytes=VMEM_LIMIT),
        name="conv_branch",
    )(a4, a4, a4, cw, cb)


def _attn_kernel(sink_ref, q_ref, kvp_ref, kvc_ref, kvn_ref, gb_ref, bias_ref, o_ref, *, qb):
    step = pl.program_id(1)
    last_blk = pl.num_programs(1) * qb - 1
    nk = 3 * BLK
    lo = lax.broadcasted_iota(jnp.int32, (nk, LANES), 1) < HEAD_DIM
    lo_q = lax.broadcasted_iota(jnp.int32, (2 * BLK, LANES), 1) < HEAD_DIM
    top = lax.broadcasted_iota(jnp.int32, (2 * BLK, 1), 0) < BLK
    zeros = jnp.zeros((nk, LANES), BF16)
    ones_lo = jnp.where(lo, 1.0, 0.0).astype(BF16)
    ones_hi = jnp.where(lo, 0.0, 1.0).astype(BF16)
    sum_cols = jnp.concatenate([ones_lo, ones_hi], axis=0)

    def keys(sub, cols):
        parts = []
        for t in (sub - 1, sub, sub + 1):
            if t < 0:
                parts.append(kvp_ref[0, :, cols])
            elif t >= qb:
                parts.append(kvn_ref[0, :, cols])
            else:
                parts.append(kvc_ref[0, t * BLK:(t + 1) * BLK, cols])
        return jnp.concatenate(parts, axis=0)

    for sub, h in [(sub, h) for sub in range(qb) for h in range(N_KV_HEADS)]:
        blk = step * qb + sub
        variant = jnp.where(blk == 0, 0, jnp.where(blk == last_blk, 2, 1))
        rows = slice(sub * BLK, (sub + 1) * BLK)
        kv = keys(sub, slice(h * HW, h * HW + LANES))
        vk = keys(sub, slice(h * HW + LANES, (h + 1) * HW))
        kblk = jnp.concatenate([jnp.where(lo, kv, zeros), jnp.where(lo, zeros, vk)], axis=0)
        vblk = jnp.concatenate([jnp.where(lo, vk, zeros), jnp.where(lo, zeros, kv)], axis=0)
        vblk = jnp.concatenate([vblk, sum_cols], axis=1)
        hs = slice(h * HW, (h + 1) * HW)
        qh = q_ref[0, rows, hs]
        lhs = jnp.concatenate([qh[:, :LANES], qh[:, LANES:]], axis=0)
        s = lax.dot_general(lhs, kblk, (((1,), (1,)), ((), ())), preferred_element_type=F32)
        s = s + bias_ref[variant, h]
        sa, sb = s[:, :nk], s[:, nk:]
        g0 = h * GQA_GROUP
        sk_a = jnp.where(top, sink_ref[g0], sink_ref[g0 + 2])
        sk_b = jnp.where(top, sink_ref[g0 + 1], sink_ref[g0 + 3])
        ma = jnp.maximum(jnp.max(sa, axis=-1, keepdims=True), sk_a)
        mb = jnp.maximum(jnp.max(sb, axis=-1, keepdims=True), sk_b)
        pe = jnp.concatenate([jnp.exp2(sa - ma), jnp.exp2(sb - mb)], axis=1).astype(BF16)
        o = jnp.dot(pe, vblk, preferred_element_type=F32)
        den = o[:, LANES:] + jnp.where(lo_q, jnp.exp2(sk_a - ma), jnp.exp2(sk_b - mb))
        y = o[:, :LANES] * (1.0 / den)
        yh = jnp.concatenate([y[:BLK], y[BLK:]], axis=1)
        o_ref[0, rows, hs] = (yh * gb_ref[0, rows, hs]).astype(BF16)


def _attn_branch(sink, q3, kv3, gb3, bias4, *, qb=8):
    B, S, _ = q3.shape
    nb = S // BLK
    kvw = kv3.shape[-1]

    def row(width=W_ATT):
        return pl.BlockSpec((1, qb * BLK, width), lambda b, j: (b, j, 0))

    def halo(d):
        return pl.BlockSpec((1, BLK, kvw),
                            lambda b, j: (b, jnp.clip(j * qb + d, 0, nb - 1), 0))

    return pl.pallas_call(
        functools.partial(_attn_kernel, qb=qb),
        grid=(B, nb // qb),
        in_specs=[
            pl.BlockSpec(memory_space=pltpu.SMEM),
            row(), halo(-1), row(kvw), halo(qb), row(),
            pl.BlockSpec(bias4.shape, lambda b, j: (0, 0, 0, 0), pipeline_mode=pl.Buffered(1)),
        ],
        out_specs=row(),
        out_shape=jax.ShapeDtypeStruct((B, S, W_ATT), BF16),
        compiler_params=pltpu.CompilerParams(
            dimension_semantics=("arbitrary", "arbitrary"),
            vmem_limit_bytes=VMEM_LIMIT),
        name="attn_branch",
    )(sink, q3, kv3, kv3, kv3, gb3, bias4)


def _outproj_kernel(h_ref, yc_ref, gz_ref, yb_ref, p_ref, lg_ref, lb_ref, woa_ref, wob_ref,
                    wpe_ref, peg_ref, wpg_ref, ng_ref, *out_refs, last):
    ys = [yc_ref[c] for c in range(N_SLAB)]
    tot = ys[0]
    for c in range(1, N_SLAB):
        tot = tot + ys[c]
    mu = jnp.sum(tot, axis=-1, keepdims=True) * (1.0 / W_CONV)
    xcs = [y - mu for y in ys]
    sq = xcs[0] * xcs[0]
    for c in range(1, N_SLAB):
        sq = sq + xcs[c] * xcs[c]
    var = jnp.sum(sq, axis=-1, keepdims=True) * (1.0 / W_CONV)
    rs = lax.rsqrt(var + EPS)
    ya = []
    for c in range(N_SLAB):
        cs = slice(c * LANES, (c + 1) * LANES)
        y = (xcs[c] * rs) * lg_ref[:, cs] + lb_ref[:, cs]
        ya.append((_silu(y) * gz_ref[:, cs]).astype(BF16))
    ya = jnp.concatenate(ya, axis=1)

    h1 = h_ref[...] + jnp.dot(yb_ref[...], wob_ref[...], preferred_element_type=F32)
    h1 = h1 + jnp.dot(ya, woa_ref[...], preferred_element_type=F32)
    e = jnp.dot(p_ref[...].astype(BF16), wpe_ref[...], preferred_element_type=F32)
    e = _rmsnorm(e, peg_ref[...])
    gate = jnp.dot(h1.astype(BF16), wpg_ref[...], preferred_element_type=F32)
    h2 = h1 + e * _sigmoid(gate)
    if last:
        (o_ref,) = out_refs
        o_ref[...] = _rmsnorm(h2, ng_ref[...])
    else:
        h_out, hn_out = out_refs
        h_out[...] = h2
        hn_out[...] = _rmsnorm(h2, ng_ref[...]).astype(BF16)


def _outproj(h, yc, gz, yb, p, lg, lb, wo, wpe, peg, wpg, ng, layer, *, last, tm=256):
    T = h.shape[0]

    def weight(rows, blk=0):
        return pl.BlockSpec((None, rows, D_MODEL), lambda i: (layer, blk, 0),
                            pipeline_mode=pl.Buffered(1))

    def row(width):
        return pl.BlockSpec((tm, width), lambda i: (i, 0))

    def const(shape):
        return pl.BlockSpec(shape, lambda i: (0, 0), pipeline_mode=pl.Buffered(1))

    if last:
        out_specs = [row(D_MODEL)]
        out_shape = [jax.ShapeDtypeStruct((T, D_MODEL), F32)]
    else:
        out_specs = [row(D_MODEL), row(D_MODEL)]
        out_shape = [jax.ShapeDtypeStruct((T, D_MODEL), F32),
                     jax.ShapeDtypeStruct((T, D_MODEL), BF16)]
    return pl.pallas_call(
        functools.partial(_outproj_kernel, last=last),
        grid=(T // tm,),
        in_specs=[row(D_MODEL),
                  pl.BlockSpec((N_SLAB, tm, LANES), lambda i: (0, i, 0)),
                  row(W_CONV), row(W_ATT),
                  pl.BlockSpec((None, tm, PLE_DIM), lambda i: (layer, i, 0)),
                  const((1, W_CONV)), const((1, W_CONV)),
                  weight(W_CONV, 0), weight(W_ATT, 1), weight(PLE_DIM),
                  const((1, D_MODEL)), weight(D_MODEL), const((1, D_MODEL))],
        out_specs=out_specs,
        out_shape=out_shape,
        compiler_params=pltpu.CompilerParams(
            dimension_semantics=("arbitrary",),
            vmem_limit_bytes=VMEM_LIMIT),
        name="outproj_final" if last else "outproj",
    )(h, yc, gz, yb, p, lg, lb, wo, wo, wpe, peg, wpg, ng)


def _band_buckets():
    q_off = np.arange(BLK)[:, None]
    k_off = np.arange(3 * BLK)[None, :] - BLK
    rel = k_off - q_off
    half = NUM_BUCKETS // 2
    ret = (rel > 0).astype(np.int32) * half
    n = np.abs(rel)
    max_exact = half // 2
    large = max_exact + (np.log(np.maximum(n, 1) / max_exact)
                         / np.log(MAX_DISTANCE / max_exact)
                         * (half - max_exact)).astype(np.int32)
    large = np.minimum(large, half - 1)
    ret = ret + np.where(n < max_exact, n, large)
    return ret.astype(np.int32), (n <= WINDOW)


def _bias_table(rel_bias):
    buckets, band = _band_buckets()
    onehot = np.zeros((BLK * 3 * BLK, NUM_BUCKETS), np.float32)
    onehot[np.arange(onehot.shape[0]), buckets.reshape(-1)] = 1.0
    bias = jnp.dot(rel_bias.astype(F32).T, jnp.asarray(onehot.T),
                   precision=lax.Precision.HIGHEST)
    bias = bias.reshape(N_Q_HEADS, BLK, 3 * BLK)
    bias = jnp.where(jnp.asarray(band)[None], bias * LOG2E, NEG)
    col = np.arange(3 * BLK)
    first = jnp.where(jnp.asarray(col < BLK)[None, None], NEG, bias)
    last = jnp.where(jnp.asarray(col >= 2 * BLK)[None, None], NEG, bias)
    tab = jnp.stack([first, bias, last])
    tab = tab.reshape(3, N_Q_HEADS // 2, 2, BLK, 3 * BLK)
    tab = jnp.concatenate([tab[:, :, 0], tab[:, :, 1]], axis=-1)
    return tab.reshape(3, N_KV_HEADS, 2 * BLK, 6 * BLK)


def _prep_w_in(w_in):
    q0 = 3 * W_CONV
    col = np.arange(w_in.shape[-1])
    scale = np.where((col >= q0) & (col < q0 + W_ATT), HEAD_DIM ** -0.5 * LOG2E, 1.0)
    return (w_in * jnp.asarray(scale, F32)).astype(BF16)


def kernel(x, p, norm_g, w_in, conv_w, conv_b, cln_g, cln_b, sink, rel_bias,
           w_out, w_pe, pe_g, w_pg, final_g):
    B, S, _ = x.shape
    T = B * S
    bias4 = _bias_table(rel_bias)
    w_all = _prep_w_in(w_in)
    wo = w_out.astype(BF16)
    wpe = w_pe.astype(BF16)
    wpg = w_pg.astype(BF16)
    cw = jnp.transpose(conv_w.reshape(DEPTH, CONV_WIDTH, N_SLAB, LANES), (0, 2, 1, 3))
    p3 = p.reshape(DEPTH, T, PLE_DIM)
    h = x.reshape(T, D_MODEL)
    hn = h
    for i in range(DEPTH):
        a, gz, gb, q, kv = _inproj(hn, norm_g[i].reshape(1, D_MODEL), w_all, i,
                                   fuse_norm=(i == 0))
        yc = _conv_branch(a.reshape(N_SLAB, B, S, LANES), cw[i],
                          conv_b[i].reshape(N_SLAB, 1, LANES))
        yb = _attn_branch(sink[i] * LOG2E, q.reshape(B, S, W_ATT),
                          kv.reshape(B, S, N_KV_HEADS * HW), gb.reshape(B, S, W_ATT), bias4)
        last = i == DEPTH - 1
        ng = final_g if last else norm_g[i + 1]
        outs = _outproj(h, yc.reshape(N_SLAB, T, LANES), gz, yb.reshape(T, W_ATT), p3,
                        cln_g[i].reshape(1, W_CONV), cln_b[i].reshape(1, W_CONV),
                        wo, wpe, pe_g[i].reshape(1, D_MODEL), wpg, ng.reshape(1, D_MODEL), i,
                        last=last)
        if last:
            (h,) = outs
        else:
            h, hn = outs
    return h.reshape(B, S, D_MODEL)
```

```python
import functools

import numpy as np
import jax
import jax.numpy as jnp
from jax import lax
from jax.experimental import pallas as pl
from jax.experimental.pallas import tpu as pltpu

D_MODEL = 2048
DEPTH = 4
W_CONV = 1024
HEAD_DIM = 64
N_Q_HEADS = 16
N_KV_HEADS = 4
GQA_GROUP = 4
W_ATT = 1024
W_KV = 256
CONV_WIDTH = 31
CONV_PAD = 15
WINDOW = 128
BLK = 128
NUM_BUCKETS = 32
MAX_DISTANCE = 128
PLE_DIM = 256
EPS = 1e-6
NEG = -1e30
LOG2E = float(np.log2(np.e))

LANES = 128
SUBLANES = 8
HALO = 16
N_SLAB = W_CONV // LANES
HW = GQA_GROUP * HEAD_DIM
MXU_N = 256
N_GROUP = 2
GW = W_ATT // N_GROUP
KVW = (2 * W_KV) // N_GROUP
assert KVW % MXU_N == 0 and GW % MXU_N == 0
VMEM_LIMIT = 56 * 1024 * 1024

F32 = jnp.float32
BF16 = jnp.bfloat16


def _sigmoid(x):
    return jax.nn.sigmoid(x)


def _silu(x):
    return x * jax.nn.sigmoid(x)


def _rmsnorm(x, g):
    ms = jnp.mean(x * x, axis=-1, keepdims=True)
    return (x * lax.rsqrt(ms + EPS)) * g


def _inproj_kernel(x_ref, g_ref, wav_ref, wag_ref, waz_ref, wbz_ref, wq_ref, wk_ref, wv_ref,
                   wo32_ref, wpg32_ref, a_ref, gz_ref, gb_ref, q_ref, kv_ref, wo_ref, wpg_ref,
                   *scratch, fuse_norm):
    wo_ref[...] = wo32_ref[...].astype(BF16)
    wpg_ref[...] = wpg32_ref[...].astype(BF16)
    if fuse_norm:
        (hn_ref,) = scratch

        @pl.when(pl.program_id(1) == 0)
        def _():
            hn_ref[...] = _rmsnorm(x_ref[...], g_ref[...]).astype(BF16)

        hn = hn_ref[...]
    else:
        hn = x_ref[...]

    def proj(w):
        return jnp.dot(hn, w, preferred_element_type=F32)

    a = proj(wav_ref[...]) * _sigmoid(proj(wag_ref[...]))
    for s in range(GW // LANES):
        a_ref[s] = a[:, s * LANES:(s + 1) * LANES]
    gz_ref[...] = _silu(proj(waz_ref[...]))
    gb_ref[...] = _silu(proj(wbz_ref[...]))
    q_ref[...] = proj(wq_ref[...]).astype(BF16)
    kv = proj(jnp.concatenate([wk_ref[...], wv_ref[...]], axis=1))
    k2, v2 = kv[:, :LANES], kv[:, LANES:]
    k2r, v2r = pltpu.roll(k2, HEAD_DIM, 1), pltpu.roll(v2, HEAD_DIM, 1)
    lo = lax.broadcasted_iota(jnp.int32, k2.shape, 1) < HEAD_DIM
    pieces = [jnp.where(lo, k2, v2r), jnp.where(lo, v2, k2r),
              jnp.where(lo, k2r, v2), jnp.where(lo, v2r, k2)]
    for s, piece in enumerate(pieces):
        kv_ref[:, s * LANES:(s + 1) * LANES] = piece.astype(BF16)


def _inproj(x, g, w_in, w_out, w_pg, layer, *, fuse_norm, tm=512):
    T = x.shape[0]
    scratch = [pltpu.VMEM((tm, D_MODEL), BF16)] if fuse_norm else []
    steps = (T // tm) * N_GROUP
    wrows = D_MODEL // steps
    assert wrows * steps == D_MODEL and wrows % 16 == 0

    def col(dtype, width=GW):
        return (pl.BlockSpec((tm, width), lambda i, j: (i, j)),
                jax.ShapeDtypeStruct((T, N_GROUP * width), dtype))

    def wcols(offset, width):
        base = offset // width
        assert base * width == offset
        return pl.BlockSpec((D_MODEL, width), lambda i, j: (0, base + j))

    cast_in = pl.BlockSpec((None, wrows, D_MODEL), lambda i, j: (layer, i * N_GROUP + j, 0))
    cast_out = (pl.BlockSpec((wrows, D_MODEL), lambda i, j: (i * N_GROUP + j, 0)),
                jax.ShapeDtypeStruct((D_MODEL, D_MODEL), BF16))

    o = np.cumsum([0, W_CONV, W_CONV, W_CONV, W_ATT, W_KV, W_KV])
    kw = W_KV // N_GROUP
    w_specs = [wcols(o[0], GW), wcols(o[1], GW), wcols(o[2], GW), wcols(o[6], GW),
               wcols(o[3], GW), wcols(o[4], kw), wcols(o[5], kw)]
    specs, shapes = zip(col(F32), col(F32), col(BF16), col(BF16, 2 * KVW), cast_out, cast_out)
    spg = GW // LANES
    return pl.pallas_call(
        functools.partial(_inproj_kernel, fuse_norm=fuse_norm),
        grid=(T // tm, N_GROUP),
        in_specs=[
            pl.BlockSpec((tm, D_MODEL), lambda i, j: (i, 0)),
            pl.BlockSpec((1, D_MODEL), lambda i, j: (0, 0)),
        ] + w_specs + [cast_in, cast_in],
        out_specs=[pl.BlockSpec((spg, tm, LANES), lambda i, j: (j, i, 0))] + list(specs),
        out_shape=[jax.ShapeDtypeStruct((N_SLAB, T, LANES), F32)] + list(shapes),
        scratch_shapes=scratch,
        compiler_params=pltpu.CompilerParams(
            dimension_semantics=("arbitrary", "arbitrary"),
            vmem_limit_bytes=VMEM_LIMIT),
        name="inproj_norm" if fuse_norm else "inproj",
    )(x, g, *([w_in] * len(w_specs)), w_out, w_pg)


def _conv_kernel(ap_ref, ac_ref, an_ref, cw_ref, cb_ref, o_ref, a_ext, *, ts, rc):
    i = pl.program_id(1)
    n = pl.num_programs(1)
    zero = jnp.zeros((N_SLAB, HALO, LANES), F32)
    a_ext[:, 0:HALO, :] = jnp.where(i > 0, ap_ref[:, 0], zero)
    a_ext[:, HALO:HALO + ts, :] = ac_ref[:, 0]
    a_ext[:, HALO + ts:, :] = jnp.where(i < n - 1, an_ref[:, 0], zero)

    n_sub = rc // SUBLANES
    first = HALO - CONV_PAD
    n_off = CONV_WIDTH + (n_sub - 1) * SUBLANES

    for c in range(N_SLAB):
        taps = [jnp.broadcast_to(cw_ref[c, k:k + 1, :], (SUBLANES, LANES))
                for k in range(CONV_WIDTH)]
        bias = jnp.broadcast_to(cb_ref[c], (SUBLANES, LANES))

        def conv_chunk(r, carry, c=c, taps=taps, bias=bias):
            r0 = pl.multiple_of(r * rc, rc)
            accs = [[bias, None] for _ in range(n_sub)]
            for o in range(n_off):
                win = a_ext[c, pl.ds(r0 + first + o, SUBLANES, stride=1), :]
                for j in range(n_sub):
                    k = o - j * SUBLANES
                    if 0 <= k < CONV_WIDTH:
                        prod = win * taps[k]
                        cur = accs[j][k % 2]
                        accs[j][k % 2] = prod if cur is None else cur + prod
            o_ref[c, 0, pl.ds(r0, rc), :] = jnp.concatenate([e + o_ for e, o_ in accs], axis=0)
            return carry

        lax.fori_loop(0, ts // rc, conv_chunk, 0)


def _conv_branch(a4, cw, cb, *, ts=1024, rc=128):
    _, B, S, _ = a4.shape
    nh = ts // HALO
    last_h = S // HALO - 1

    def const(shape):
        return pl.BlockSpec(shape, lambda b, i: (0,) * len(shape))

    return pl.pallas_call(
        functools.partial(_conv_kernel, ts=ts, rc=rc),
        grid=(B, S // ts),
        in_specs=[
            pl.BlockSpec((N_SLAB, 1, HALO, LANES),
                         lambda b, i: (0, b, jnp.maximum(i * nh - 1, 0), 0)),
            pl.BlockSpec((N_SLAB, 1, ts, LANES), lambda b, i: (0, b, i, 0)),
            pl.BlockSpec((N_SLAB, 1, HALO, LANES),
                         lambda b, i: (0, b, jnp.minimum((i + 1) * nh, last_h), 0)),
            const((N_SLAB, CONV_WIDTH, LANES)), const((N_SLAB, 1, LANES))],
        out_specs=pl.BlockSpec((N_SLAB, 1, ts, LANES), lambda b, i: (0, b, i, 0)),
        out_shape=jax.ShapeDtypeStruct((N_SLAB, B, S, LANES), F32),
        scratch_shapes=[pltpu.VMEM((N_SLAB, ts + 2 * HALO, LANES), F32)],
        compiler_params=pltpu.CompilerParams(
            dimension_semantics=("arbitrary", "arbitrary"),
            vmem_limit_bytes=VMEM_LIMIT),
        name="conv_branch",
    )(a4, a4, a4, cw, cb)


def _attn_kernel(sink_ref, q_ref, kvp_ref, kvc_ref, kvn_ref, gb_ref, bias_ref, o_ref, *, qb):
    step = pl.program_id(1)
    last_blk = pl.num_programs(1) * qb - 1
    nk = 3 * BLK
    lo = lax.broadcasted_iota(jnp.int32, (nk, LANES), 1) < HEAD_DIM
    lo_q = lax.broadcasted_iota(jnp.int32, (2 * BLK, LANES), 1) < HEAD_DIM
    top = lax.broadcasted_iota(jnp.int32, (2 * BLK, 1), 0) < BLK
    zeros = jnp.zeros((nk, LANES), BF16)
    ones_lo = jnp.where(lo, 1.0, 0.0).astype(BF16)
    ones_hi = jnp.where(lo, 0.0, 1.0).astype(BF16)
    sum_cols = jnp.concatenate([ones_lo, ones_hi], axis=0)

    def keys(sub, cols):
        parts = []
        for t in (sub - 1, sub, sub + 1):
            if t < 0:
                parts.append(kvp_ref[0, :, cols])
            elif t >= qb:
                parts.append(kvn_ref[0, :, cols])
            else:
                parts.append(kvc_ref[0, t * BLK:(t + 1) * BLK, cols])
        return jnp.concatenate(parts, axis=0)

    for sub, h in [(sub, h) for sub in range(qb) for h in range(N_KV_HEADS)]:
        blk = step * qb + sub
        variant = jnp.where(blk == 0, 0, jnp.where(blk == last_blk, 2, 1))
        rows = slice(sub * BLK, (sub + 1) * BLK)
        kv = keys(sub, slice(h * HW, h * HW + LANES))
        vk = keys(sub, slice(h * HW + LANES, (h + 1) * HW))
        kblk = jnp.concatenate([jnp.where(lo, kv, zeros), jnp.where(lo, zeros, vk)], axis=0)
        vblk = jnp.concatenate([jnp.where(lo, vk, zeros), jnp.where(lo, zeros, kv)], axis=0)
        vblk = jnp.concatenate([vblk, sum_cols], axis=1)
        hs = slice(h * HW, (h + 1) * HW)
        qh = q_ref[0, rows, hs]
        lhs = jnp.concatenate([qh[:, :LANES], qh[:, LANES:]], axis=0)
        s = lax.dot_general(lhs, kblk, (((1,), (1,)), ((), ())), preferred_element_type=F32)
        s = s + bias_ref[variant, h]
        sa, sb = s[:, :nk], s[:, nk:]
        g0 = h * GQA_GROUP
        sk_a = jnp.where(top, sink_ref[g0], sink_ref[g0 + 2])
        sk_b = jnp.where(top, sink_ref[g0 + 1], sink_ref[g0 + 3])
        ma = jnp.maximum(jnp.max(sa, axis=-1, keepdims=True), sk_a)
        mb = jnp.maximum(jnp.max(sb, axis=-1, keepdims=True), sk_b)
        pe = jnp.concatenate([jnp.exp2(sa - ma), jnp.exp2(sb - mb)], axis=1).astype(BF16)
        o = jnp.dot(pe, vblk, preferred_element_type=F32)
        den = o[:, LANES:] + jnp.where(lo_q, jnp.exp2(sk_a - ma), jnp.exp2(sk_b - mb))
        y = o[:, :LANES] * (1.0 / den)
        yh = jnp.concatenate([y[:BLK], y[BLK:]], axis=1)
        o_ref[0, rows, hs] = (yh * gb_ref[0, rows, hs]).astype(BF16)


def _attn_branch(sink, q3, kv3, gb3, bias4, *, qb=8):
    B, S, _ = q3.shape
    nb = S // BLK
    kvw = kv3.shape[-1]

    def row(width=W_ATT):
        return pl.BlockSpec((1, qb * BLK, width), lambda b, j: (b, j, 0))

    def halo(d):
        return pl.BlockSpec((1, BLK, kvw),
                            lambda b, j: (b, jnp.clip(j * qb + d, 0, nb - 1), 0))

    return pl.pallas_call(
        functools.partial(_attn_kernel, qb=qb),
        grid=(B, nb // qb),
        in_specs=[
            pl.BlockSpec(memory_space=pltpu.SMEM),
            row(), halo(-1), row(kvw), halo(qb), row(),
            pl.BlockSpec(bias4.shape, lambda b, j: (0, 0, 0, 0), pipeline_mode=pl.Buffered(1)),
        ],
        out_specs=row(),
        out_shape=jax.ShapeDtypeStruct((B, S, W_ATT), BF16),
        compiler_params=pltpu.CompilerParams(
            dimension_semantics=("arbitrary", "arbitrary"),
            vmem_limit_bytes=VMEM_LIMIT),
        name="attn_branch",
    )(sink, q3, kv3, kv3, kv3, gb3, bias4)


def _outproj_kernel(h_ref, yc_ref, gz_ref, yb_ref, p_ref, lg_ref, lb_ref, woa_ref, wob_ref,
                    wpe_ref, peg_ref, wpg_ref, ng_ref, *rest, last):
    if last:
        (o_ref,) = rest
    else:
        win32_ref, scale_ref, h_out, hn_out, win_ref = rest
        win_ref[...] = (win32_ref[...] * scale_ref[...]).astype(BF16)
    ys = [yc_ref[c] for c in range(N_SLAB)]
    tot = ys[0]
    for c in range(1, N_SLAB):
        tot = tot + ys[c]
    mu = jnp.sum(tot, axis=-1, keepdims=True) * (1.0 / W_CONV)
    xcs = [y - mu for y in ys]
    sq = xcs[0] * xcs[0]
    for c in range(1, N_SLAB):
        sq = sq + xcs[c] * xcs[c]
    var = jnp.sum(sq, axis=-1, keepdims=True) * (1.0 / W_CONV)
    rs = lax.rsqrt(var + EPS)
    ya = []
    for c in range(N_SLAB):
        cs = slice(c * LANES, (c + 1) * LANES)
        y = (xcs[c] * rs) * lg_ref[:, cs] + lb_ref[:, cs]
        ya.append((_silu(y) * gz_ref[:, cs]).astype(BF16))
    ya = jnp.concatenate(ya, axis=1)

    h1 = h_ref[...] + jnp.dot(yb_ref[...], wob_ref[...], preferred_element_type=F32)
    h1 = h1 + jnp.dot(ya, woa_ref[...], preferred_element_type=F32)
    e = jnp.dot(p_ref[...].astype(BF16), wpe_ref[...], preferred_element_type=F32)
    e = _rmsnorm(e, peg_ref[...])
    gate = jnp.dot(h1.astype(BF16), wpg_ref[...], preferred_element_type=F32)
    h2 = h1 + e * _sigmoid(gate)
    if last:
        o_ref[...] = _rmsnorm(h2, ng_ref[...])
    else:
        h_out[...] = h2
        hn_out[...] = _rmsnorm(h2, ng_ref[...]).astype(BF16)


def _outproj(h, yc, gz, yb, p, lg, lb, wo, wpe, peg, wpg, ng, w_in32, w_scale, layer, *, last,
             tm=256):
    T = h.shape[0]
    steps = T // tm
    wrows = D_MODEL // steps
    assert wrows * steps == D_MODEL and wrows % 16 == 0
    w_in_cols = w_in32.shape[-1]

    def row(width):
        return pl.BlockSpec((tm, width), lambda i: (i, 0))

    def const(shape, blk=0):
        return pl.BlockSpec(shape, lambda i: (blk, 0), pipeline_mode=pl.Buffered(1))

    in_specs = [row(D_MODEL),
                pl.BlockSpec((N_SLAB, tm, LANES), lambda i: (0, i, 0)),
                row(W_CONV), row(W_ATT),
                pl.BlockSpec((None, tm, PLE_DIM), lambda i: (layer, i, 0)),
                const((1, W_CONV)), const((1, W_CONV)),
                const((W_CONV, D_MODEL), 0), const((W_ATT, D_MODEL), 1),
                pl.BlockSpec((None, PLE_DIM, D_MODEL), lambda i: (layer, 0, 0),
                             pipeline_mode=pl.Buffered(1)),
                const((1, D_MODEL)), const((D_MODEL, D_MODEL)), const((1, D_MODEL))]
    args = [h, yc, gz, yb, p, lg, lb, wo, wo, wpe, peg, wpg, ng]
    if last:
        out_specs = [row(D_MODEL)]
        out_shape = [jax.ShapeDtypeStruct((T, D_MODEL), F32)]
    else:
        in_specs += [pl.BlockSpec((None, wrows, w_in_cols), lambda i: (layer + 1, i, 0)),
                     const((1, w_in_cols))]
        args += [w_in32, w_scale]
        out_specs = [row(D_MODEL), row(D_MODEL),
                     pl.BlockSpec((wrows, w_in_cols), lambda i: (i, 0))]
        out_shape = [jax.ShapeDtypeStruct((T, D_MODEL), F32),
                     jax.ShapeDtypeStruct((T, D_MODEL), BF16),
                     jax.ShapeDtypeStruct((D_MODEL, w_in_cols), BF16)]
    return pl.pallas_call(
        functools.partial(_outproj_kernel, last=last),
        grid=(steps,),
        in_specs=in_specs,
        out_specs=out_specs,
        out_shape=out_shape,
        compiler_params=pltpu.CompilerParams(
            dimension_semantics=("arbitrary",),
            vmem_limit_bytes=VMEM_LIMIT),
        name="outproj_final" if last else "outproj",
    )(*args)


def _band_buckets():
    q_off = np.arange(BLK)[:, None]
    k_off = np.arange(3 * BLK)[None, :] - BLK
    rel = k_off - q_off
    half = NUM_BUCKETS // 2
    ret = (rel > 0).astype(np.int32) * half
    n = np.abs(rel)
    max_exact = half // 2
    large = max_exact + (np.log(np.maximum(n, 1) / max_exact)
                         / np.log(MAX_DISTANCE / max_exact)
                         * (half - max_exact)).astype(np.int32)
    large = np.minimum(large, half - 1)
    ret = ret + np.where(n < max_exact, n, large)
    return ret.astype(np.int32), (n <= WINDOW)


def _bias_table(rel_bias):
    buckets, band = _band_buckets()
    onehot = np.zeros((BLK * 3 * BLK, NUM_BUCKETS), np.float32)
    onehot[np.arange(onehot.shape[0]), buckets.reshape(-1)] = 1.0
    bias = jnp.dot(rel_bias.astype(F32).T, jnp.asarray(onehot.T),
                   precision=lax.Precision.HIGHEST)
    bias = bias.reshape(N_Q_HEADS, BLK, 3 * BLK)
    bias = jnp.where(jnp.asarray(band)[None], bias * LOG2E, NEG)
    col = np.arange(3 * BLK)
    first = jnp.where(jnp.asarray(col < BLK)[None, None], NEG, bias)
    last = jnp.where(jnp.asarray(col >= 2 * BLK)[None, None], NEG, bias)
    tab = jnp.stack([first, bias, last])
    tab = tab.reshape(3, N_Q_HEADS // 2, 2, BLK, 3 * BLK)
    tab = jnp.concatenate([tab[:, :, 0], tab[:, :, 1]], axis=-1)
    return tab.reshape(3, N_KV_HEADS, 2 * BLK, 6 * BLK)


def _w_in_scale():
    q0 = 3 * W_CONV
    col = np.arange(3 * W_CONV + 2 * W_ATT + 2 * W_KV)
    scale = np.where((col >= q0) & (col < q0 + W_ATT), HEAD_DIM ** -0.5 * LOG2E, 1.0)
    return jnp.asarray(scale, F32).reshape(1, -1)


def kernel(x, p, norm_g, w_in, conv_w, conv_b, cln_g, cln_b, sink, rel_bias,
           w_out, w_pe, pe_g, w_pg, final_g):
    B, S, _ = x.shape
    T = B * S
    bias4 = _bias_table(rel_bias)
    w_scale = _w_in_scale()
    win = (w_in[0] * w_scale).astype(BF16)
    wpe = w_pe.astype(BF16)
    cw = jnp.transpose(conv_w.reshape(DEPTH, CONV_WIDTH, N_SLAB, LANES), (0, 2, 1, 3))
    p3 = p.reshape(DEPTH, T, PLE_DIM)
    h = x.reshape(T, D_MODEL)
    hn = h
    for i in range(DEPTH):
        a, gz, gb, q, kv, wo, wpg = _inproj(hn, norm_g[i].reshape(1, D_MODEL), win, w_out, w_pg,
                                            i, fuse_norm=(i == 0))
        yc = _conv_branch(a.reshape(N_SLAB, B, S, LANES), cw[i],
                          conv_b[i].reshape(N_SLAB, 1, LANES))
        yb = _attn_branch(sink[i] * LOG2E, q.reshape(B, S, W_ATT),
                          kv.reshape(B, S, N_KV_HEADS * HW), gb.reshape(B, S, W_ATT), bias4)
        last = i == DEPTH - 1
        ng = final_g if last else norm_g[i + 1]
        outs = _outproj(h, yc.reshape(N_SLAB, T, LANES), gz, yb.reshape(T, W_ATT), p3,
                        cln_g[i].reshape(1, W_CONV), cln_b[i].reshape(1, W_CONV),
                        wo, wpe, pe_g[i].reshape(1, D_MODEL), wpg, ng.reshape(1, D_MODEL),
                        w_in, w_scale, i, last=last)
        if last:
            (h,) = outs
        else:
            h, hn, win = outs
    return h.reshape(B, S, D_MODEL)
```

```python
import functools

import numpy as np
import jax
import jax.numpy as jnp
from jax import lax
from jax.experimental import pallas as pl
from jax.experimental.pallas import tpu as pltpu

D_MODEL = 2048
DEPTH = 4
W_CONV = 1024
HEAD_DIM = 64
N_Q_HEADS = 16
N_KV_HEADS = 4
GQA_GROUP = 4
W_ATT = 1024
W_KV = 256
CONV_WIDTH = 31
CONV_PAD = 15
WINDOW = 128
BLK = 128
NUM_BUCKETS = 32
MAX_DISTANCE = 128
PLE_DIM = 256
EPS = 1e-6
NEG = -1e30
LOG2E = float(np.log2(np.e))

LANES = 128
SUBLANES = 8
HALO = 16
N_SLAB = W_CONV // LANES
HW = GQA_GROUP * HEAD_DIM
MXU_N = 256
N_GROUP = 2
GW = W_ATT // N_GROUP
KVW = (2 * W_KV) // N_GROUP
assert KVW % MXU_N == 0 and GW % MXU_N == 0
VMEM_LIMIT = 56 * 1024 * 1024

F32 = jnp.float32
BF16 = jnp.bfloat16


def _sigmoid(x):
    return jax.nn.sigmoid(x)


def _silu(x):
    return x * jax.nn.sigmoid(x)


def _rmsnorm(x, g):
    ms = jnp.mean(x * x, axis=-1, keepdims=True)
    return (x * lax.rsqrt(ms + EPS)) * g


def _inproj_kernel(x_ref, g_ref, wav_ref, wag_ref, waz_ref, wbz_ref, wq_ref, wk_ref, wv_ref,
                   wo32_ref, wpg32_ref, a_ref, gz_ref, gb_ref, q_ref, kv_ref, wo_ref, wpg_ref,
                   *scratch, fuse_norm):
    wo_ref[...] = wo32_ref[...].astype(BF16)
    wpg_ref[...] = wpg32_ref[...].astype(BF16)
    if fuse_norm:
        (hn_ref,) = scratch

        @pl.when(pl.program_id(1) == 0)
        def _():
            hn_ref[...] = _rmsnorm(x_ref[...], g_ref[...]).astype(BF16)

        hn = hn_ref[...]
    else:
        hn = x_ref[...]

    def proj(w):
        return jnp.dot(hn, w, preferred_element_type=F32)

    a = proj(wav_ref[...]) * _sigmoid(proj(wag_ref[...]))
    for s in range(GW // LANES):
        a_ref[s] = a[:, s * LANES:(s + 1) * LANES]
    gz_ref[...] = _silu(proj(waz_ref[...]))
    gb_ref[...] = _silu(proj(wbz_ref[...]))
    q_ref[...] = proj(wq_ref[...]).astype(BF16)
    kv = proj(jnp.concatenate([wk_ref[...], wv_ref[...]], axis=1))
    k2, v2 = kv[:, :LANES], kv[:, LANES:]
    k2r, v2r = pltpu.roll(k2, HEAD_DIM, 1), pltpu.roll(v2, HEAD_DIM, 1)
    lo = lax.broadcasted_iota(jnp.int32, k2.shape, 1) < HEAD_DIM
    pieces = [jnp.where(lo, k2, v2r), jnp.where(lo, v2, k2r),
              jnp.where(lo, k2r, v2), jnp.where(lo, v2r, k2)]
    for s, piece in enumerate(pieces):
        kv_ref[:, s * LANES:(s + 1) * LANES] = piece.astype(BF16)


def _inproj(x, g, w_in, w_out, w_pg, layer, *, fuse_norm, tm=None):
    T = x.shape[0]
    if tm is None:
        tm = 512 if fuse_norm else 1024
    scratch = [pltpu.VMEM((tm, D_MODEL), BF16)] if fuse_norm else []
    steps = (T // tm) * N_GROUP
    wrows = D_MODEL // steps
    assert wrows * steps == D_MODEL and wrows % 16 == 0

    def col(dtype, width=GW):
        return (pl.BlockSpec((tm, width), lambda i, j: (i, j)),
                jax.ShapeDtypeStruct((T, N_GROUP * width), dtype))

    def wcols(offset, width):
        base = offset // width
        assert base * width == offset
        return pl.BlockSpec((D_MODEL, width), lambda i, j: (0, base + j))

    cast_in = pl.BlockSpec((None, wrows, D_MODEL), lambda i, j: (layer, i * N_GROUP + j, 0))
    cast_out = (pl.BlockSpec((wrows, D_MODEL), lambda i, j: (i * N_GROUP + j, 0)),
                jax.ShapeDtypeStruct((D_MODEL, D_MODEL), BF16))

    o = np.cumsum([0, W_CONV, W_CONV, W_CONV, W_ATT, W_KV, W_KV])
    kw = W_KV // N_GROUP
    w_specs = [wcols(o[0], GW), wcols(o[1], GW), wcols(o[2], GW), wcols(o[6], GW),
               wcols(o[3], GW), wcols(o[4], kw), wcols(o[5], kw)]
    specs, shapes = zip(col(F32), col(F32), col(BF16), col(BF16, 2 * KVW), cast_out, cast_out)
    spg = GW // LANES
    return pl.pallas_call(
        functools.partial(_inproj_kernel, fuse_norm=fuse_norm),
        grid=(T // tm, N_GROUP),
        in_specs=[
            pl.BlockSpec((tm, D_MODEL), lambda i, j: (i, 0)),
            pl.BlockSpec((1, D_MODEL), lambda i, j: (0, 0)),
        ] + w_specs + [cast_in, cast_in],
        out_specs=[pl.BlockSpec((spg, tm, LANES), lambda i, j: (j, i, 0))] + list(specs),
        out_shape=[jax.ShapeDtypeStruct((N_SLAB, T, LANES), F32)] + list(shapes),
        scratch_shapes=scratch,
        compiler_params=pltpu.CompilerParams(
            dimension_semantics=("arbitrary", "arbitrary"),
            vmem_limit_bytes=VMEM_LIMIT),
        name="inproj_norm" if fuse_norm else "inproj",
    )(x, g, *([w_in] * len(w_specs)), w_out, w_pg)


def _conv_kernel(ap_ref, ac_ref, an_ref, cw_ref, cb_ref, o_ref, a_ext, *, ts, rc):
    i = pl.program_id(1)
    n = pl.num_programs(1)
    zero = jnp.zeros((N_SLAB, HALO, LANES), F32)
    a_ext[:, 0:HALO, :] = jnp.where(i > 0, ap_ref[:, 0], zero)
    a_ext[:, HALO:HALO + ts, :] = ac_ref[:, 0]
    a_ext[:, HALO + ts:, :] = jnp.where(i < n - 1, an_ref[:, 0], zero)

    n_sub = rc // SUBLANES
    first = HALO - CONV_PAD
    n_off = CONV_WIDTH + (n_sub - 1) * SUBLANES

    for c in range(N_SLAB):
        taps = [jnp.broadcast_to(cw_ref[c, k:k + 1, :], (SUBLANES, LANES))
                for k in range(CONV_WIDTH)]
        bias = jnp.broadcast_to(cb_ref[c], (SUBLANES, LANES))

        def conv_chunk(r, carry, c=c, taps=taps, bias=bias):
            r0 = pl.multiple_of(r * rc, rc)
            accs = [[bias, None] for _ in range(n_sub)]
            for o in range(n_off):
                win = a_ext[c, pl.ds(r0 + first + o, SUBLANES, stride=1), :]
                for j in range(n_sub):
                    k = o - j * SUBLANES
                    if 0 <= k < CONV_WIDTH:
                        prod = win * taps[k]
                        cur = accs[j][k % 2]
                        accs[j][k % 2] = prod if cur is None else cur + prod
            o_ref[c, 0, pl.ds(r0, rc), :] = jnp.concatenate([e + o_ for e, o_ in accs], axis=0)
            return carry

        lax.fori_loop(0, ts // rc, conv_chunk, 0)


def _conv_branch(a4, cw, cb, *, ts=1024, rc=128):
    _, B, S, _ = a4.shape
    nh = ts // HALO
    last_h = S // HALO - 1

    def const(shape):
        return pl.BlockSpec(shape, lambda b, i: (0,) * len(shape))

    return pl.pallas_call(
        functools.partial(_conv_kernel, ts=ts, rc=rc),
        grid=(B, S // ts),
        in_specs=[
            pl.BlockSpec((N_SLAB, 1, HALO, LANES),
                         lambda b, i: (0, b, jnp.maximum(i * nh - 1, 0), 0)),
            pl.BlockSpec((N_SLAB, 1, ts, LANES), lambda b, i: (0, b, i, 0)),
            pl.BlockSpec((N_SLAB, 1, HALO, LANES),
                         lambda b, i: (0, b, jnp.minimum((i + 1) * nh, last_h), 0)),
            const((N_SLAB, CONV_WIDTH, LANES)), const((N_SLAB, 1, LANES))],
        out_specs=pl.BlockSpec((N_SLAB, 1, ts, LANES), lambda b, i: (0, b, i, 0)),
        out_shape=jax.ShapeDtypeStruct((N_SLAB, B, S, LANES), F32),
        scratch_shapes=[pltpu.VMEM((N_SLAB, ts + 2 * HALO, LANES), F32)],
        compiler_params=pltpu.CompilerParams(
            dimension_semantics=("arbitrary", "arbitrary"),
            vmem_limit_bytes=VMEM_LIMIT),
        name="conv_branch",
    )(a4, a4, a4, cw, cb)


def _attn_kernel(sink_ref, q_ref, kvp_ref, kvc_ref, kvn_ref, gb_ref, bias_ref, o_ref, *, qb):
    step = pl.program_id(1)
    last_blk = pl.num_programs(1) * qb - 1
    nk = 3 * BLK
    lo = lax.broadcasted_iota(jnp.int32, (nk, LANES), 1) < HEAD_DIM
    lo_q = lax.broadcasted_iota(jnp.int32, (2 * BLK, LANES), 1) < HEAD_DIM
    top = lax.broadcasted_iota(jnp.int32, (2 * BLK, 1), 0) < BLK
    zeros = jnp.zeros((nk, LANES), BF16)
    ones_lo = jnp.where(lo, 1.0, 0.0).astype(BF16)
    ones_hi = jnp.where(lo, 0.0, 1.0).astype(BF16)
    sum_cols = jnp.concatenate([ones_lo, ones_hi], axis=0)

    def keys(sub, cols):
        parts = []
        for t in (sub - 1, sub, sub + 1):
            if t < 0:
                parts.append(kvp_ref[0, :, cols])
            elif t >= qb:
                parts.append(kvn_ref[0, :, cols])
            else:
                parts.append(kvc_ref[0, t * BLK:(t + 1) * BLK, cols])
        return jnp.concatenate(parts, axis=0)

    for sub, h in [(sub, h) for sub in range(qb) for h in range(N_KV_HEADS)]:
        blk = step * qb + sub
        variant = jnp.where(blk == 0, 0, jnp.where(blk == last_blk, 2, 1))
        rows = slice(sub * BLK, (sub + 1) * BLK)
        kv = keys(sub, slice(h * HW, h * HW + LANES))
        vk = keys(sub, slice(h * HW + LANES, (h + 1) * HW))
        kblk = jnp.concatenate([jnp.where(lo, kv, zeros), jnp.where(lo, zeros, vk)], axis=0)
        vblk = jnp.concatenate([jnp.where(lo, vk, zeros), jnp.where(lo, zeros, kv)], axis=0)
        vblk = jnp.concatenate([vblk, sum_cols], axis=1)
        hs = slice(h * HW, (h + 1) * HW)
        qh = q_ref[0, rows, hs]
        lhs = jnp.concatenate([qh[:, :LANES], qh[:, LANES:]], axis=0)
        s = lax.dot_general(lhs, kblk, (((1,), (1,)), ((), ())), preferred_element_type=F32)
        s = s + bias_ref[variant, h]
        sa, sb = s[:, :nk], s[:, nk:]
        g0 = h * GQA_GROUP
        sk_a = jnp.where(top, sink_ref[g0], sink_ref[g0 + 2])
        sk_b = jnp.where(top, sink_ref[g0 + 1], sink_ref[g0 + 3])
        ma = jnp.maximum(jnp.max(sa, axis=-1, keepdims=True), sk_a)
        mb = jnp.maximum(jnp.max(sb, axis=-1, keepdims=True), sk_b)
        pe = jnp.concatenate([jnp.exp2(sa - ma), jnp.exp2(sb - mb)], axis=1).astype(BF16)
        o = jnp.dot(pe, vblk, preferred_element_type=F32)
        den = o[:, LANES:] + jnp.where(lo_q, jnp.exp2(sk_a - ma), jnp.exp2(sk_b - mb))
        y = o[:, :LANES] * (1.0 / den)
        yh = jnp.concatenate([y[:BLK], y[BLK:]], axis=1)
        o_ref[0, rows, hs] = (yh * gb_ref[0, rows, hs]).astype(BF16)


def _attn_branch(sink, q3, kv3, gb3, bias4, *, qb=8):
    B, S, _ = q3.shape
    nb = S // BLK
    kvw = kv3.shape[-1]

    def row(width=W_ATT):
        return pl.BlockSpec((1, qb * BLK, width), lambda b, j: (b, j, 0))

    def halo(d):
        return pl.BlockSpec((1, BLK, kvw),
                            lambda b, j: (b, jnp.clip(j * qb + d, 0, nb - 1), 0))

    return pl.pallas_call(
        functools.partial(_attn_kernel, qb=qb),
        grid=(B, nb // qb),
        in_specs=[
            pl.BlockSpec(memory_space=pltpu.SMEM),
            row(), halo(-1), row(kvw), halo(qb), row(),
            pl.BlockSpec(bias4.shape, lambda b, j: (0, 0, 0, 0), pipeline_mode=pl.Buffered(1)),
        ],
        out_specs=row(),
        out_shape=jax.ShapeDtypeStruct((B, S, W_ATT), BF16),
        compiler_params=pltpu.CompilerParams(
            dimension_semantics=("arbitrary", "arbitrary"),
            vmem_limit_bytes=VMEM_LIMIT),
        name="attn_branch",
    )(sink, q3, kv3, kv3, kv3, gb3, bias4)


def _outproj_kernel(h_ref, yc_ref, gz_ref, yb_ref, p_ref, lg_ref, lb_ref, woa_ref, wob_ref,
                    wpe_ref, peg_ref, wpg_ref, ng_ref, *rest, last):
    if last:
        (o_ref,) = rest
    else:
        win32_ref, scale_ref, h_out, hn_out, win_ref = rest
        win_ref[...] = (win32_ref[...] * scale_ref[...]).astype(BF16)
    ys = [yc_ref[c] for c in range(N_SLAB)]
    tot = ys[0]
    for c in range(1, N_SLAB):
        tot = tot + ys[c]
    mu = jnp.sum(tot, axis=-1, keepdims=True) * (1.0 / W_CONV)
    xcs = [y - mu for y in ys]
    sq = xcs[0] * xcs[0]
    for c in range(1, N_SLAB):
        sq = sq + xcs[c] * xcs[c]
    var = jnp.sum(sq, axis=-1, keepdims=True) * (1.0 / W_CONV)
    rs = lax.rsqrt(var + EPS)
    ya = []
    for c in range(N_SLAB):
        cs = slice(c * LANES, (c + 1) * LANES)
        y = (xcs[c] * rs) * lg_ref[:, cs] + lb_ref[:, cs]
        ya.append((_silu(y) * gz_ref[:, cs]).astype(BF16))
    ya = jnp.concatenate(ya, axis=1)

    h1 = h_ref[...] + jnp.dot(yb_ref[...], wob_ref[...], preferred_element_type=F32)
    h1 = h1 + jnp.dot(ya, woa_ref[...], preferred_element_type=F32)
    e = jnp.dot(p_ref[...].astype(BF16), wpe_ref[...], preferred_element_type=F32)
    e = _rmsnorm(e, peg_ref[...])
    gate = jnp.dot(h1.astype(BF16), wpg_ref[...], preferred_element_type=F32)
    h2 = h1 + e * _sigmoid(gate)
    if last:
        o_ref[...] = _rmsnorm(h2, ng_ref[...])
    else:
        h_out[...] = h2
        hn_out[...] = _rmsnorm(h2, ng_ref[...]).astype(BF16)


def _outproj(h, yc, gz, yb, p, lg, lb, wo, wpe, peg, wpg, ng, w_in32, w_scale, layer, *, last,
             tm=256):
    T = h.shape[0]
    steps = T // tm
    wrows = D_MODEL // steps
    assert wrows * steps == D_MODEL and wrows % 16 == 0
    w_in_cols = w_in32.shape[-1]

    def row(width):
        return pl.BlockSpec((tm, width), lambda i: (i, 0))

    def const(shape, blk=0):
        return pl.BlockSpec(shape, lambda i: (blk, 0), pipeline_mode=pl.Buffered(1))

    in_specs = [row(D_MODEL),
                pl.BlockSpec((N_SLAB, tm, LANES), lambda i: (0, i, 0)),
                row(W_CONV), row(W_ATT),
                pl.BlockSpec((None, tm, PLE_DIM), lambda i: (layer, i, 0)),
                const((1, W_CONV)), const((1, W_CONV)),
                const((W_CONV, D_MODEL), 0), const((W_ATT, D_MODEL), 1),
                pl.BlockSpec((None, PLE_DIM, D_MODEL), lambda i: (layer, 0, 0),
                             pipeline_mode=pl.Buffered(1)),
                const((1, D_MODEL)), const((D_MODEL, D_MODEL)), const((1, D_MODEL))]
    args = [h, yc, gz, yb, p, lg, lb, wo, wo, wpe, peg, wpg, ng]
    if last:
        out_specs = [row(D_MODEL)]
        out_shape = [jax.ShapeDtypeStruct((T, D_MODEL), F32)]
    else:
        in_specs += [pl.BlockSpec((None, wrows, w_in_cols), lambda i: (layer + 1, i, 0)),
                     const((1, w_in_cols))]
        args += [w_in32, w_scale]
        out_specs = [row(D_MODEL), row(D_MODEL),
                     pl.BlockSpec((wrows, w_in_cols), lambda i: (i, 0))]
        out_shape = [jax.ShapeDtypeStruct((T, D_MODEL), F32),
                     jax.ShapeDtypeStruct((T, D_MODEL), BF16),
                     jax.ShapeDtypeStruct((D_MODEL, w_in_cols), BF16)]
    return pl.pallas_call(
        functools.partial(_outproj_kernel, last=last),
        grid=(steps,),
        in_specs=in_specs,
        out_specs=out_specs,
        out_shape=out_shape,
        compiler_params=pltpu.CompilerParams(
            dimension_semantics=("arbitrary",),
            vmem_limit_bytes=VMEM_LIMIT),
        name="outproj_final" if last else "outproj",
    )(*args)


def _band_buckets():
    q_off = np.arange(BLK)[:, None]
    k_off = np.arange(3 * BLK)[None, :] - BLK
    rel = k_off - q_off
    half = NUM_BUCKETS // 2
    ret = (rel > 0).astype(np.int32) * half
    n = np.abs(rel)
    max_exact = half // 2
    large = max_exact + (np.log(np.maximum(n, 1) / max_exact)
                         / np.log(MAX_DISTANCE / max_exact)
                         * (half - max_exact)).astype(np.int32)
    large = np.minimum(large, half - 1)
    ret = ret + np.where(n < max_exact, n, large)
    return ret.astype(np.int32), (n <= WINDOW)


def _bias_table(rel_bias):
    buckets, band = _band_buckets()
    onehot = np.zeros((BLK * 3 * BLK, NUM_BUCKETS), np.float32)
    onehot[np.arange(onehot.shape[0]), buckets.reshape(-1)] = 1.0
    bias = jnp.dot(rel_bias.astype(F32).T, jnp.asarray(onehot.T),
                   precision=lax.Precision.HIGHEST)
    bias = bias.reshape(N_Q_HEADS, BLK, 3 * BLK)
    bias = jnp.where(jnp.asarray(band)[None], bias * LOG2E, NEG)
    col = np.arange(3 * BLK)
    first = jnp.where(jnp.asarray(col < BLK)[None, None], NEG, bias)
    last = jnp.where(jnp.asarray(col >= 2 * BLK)[None, None], NEG, bias)
    tab = jnp.stack([first, bias, last])
    tab = tab.reshape(3, N_Q_HEADS // 2, 2, BLK, 3 * BLK)
    tab = jnp.concatenate([tab[:, :, 0], tab[:, :, 1]], axis=-1)
    return tab.reshape(3, N_KV_HEADS, 2 * BLK, 6 * BLK)


def _w_in_scale():
    q0 = 3 * W_CONV
    col = np.arange(3 * W_CONV + 2 * W_ATT + 2 * W_KV)
    scale = np.where((col >= q0) & (col < q0 + W_ATT), HEAD_DIM ** -0.5 * LOG2E, 1.0)
    return jnp.asarray(scale, F32).reshape(1, -1)


def kernel(x, p, norm_g, w_in, conv_w, conv_b, cln_g, cln_b, sink, rel_bias,
           w_out, w_pe, pe_g, w_pg, final_g):
    B, S, _ = x.shape
    T = B * S
    bias4 = _bias_table(rel_bias)
    w_scale = _w_in_scale()
    win = (w_in[0] * w_scale).astype(BF16)
    wpe = w_pe.astype(BF16)
    cw = jnp.transpose(conv_w.reshape(DEPTH, CONV_WIDTH, N_SLAB, LANES), (0, 2, 1, 3))
    p3 = p.reshape(DEPTH, T, PLE_DIM)
    h = x.reshape(T, D_MODEL)
    hn = h
    for i in range(DEPTH):
        a, gz, gb, q, kv, wo, wpg = _inproj(hn, norm_g[i].reshape(1, D_MODEL), win, w_out, w_pg,
                                            i, fuse_norm=(i == 0))
        yc = _conv_branch(a.reshape(N_SLAB, B, S, LANES), cw[i],
                          conv_b[i].reshape(N_SLAB, 1, LANES))
        yb = _attn_branch(sink[i] * LOG2E, q.reshape(B, S, W_ATT),
                          kv.reshape(B, S, N_KV_HEADS * HW), gb.reshape(B, S, W_ATT), bias4)
        last = i == DEPTH - 1
        ng = final_g if last else norm_g[i + 1]
        outs = _outproj(h, yc.reshape(N_SLAB, T, LANES), gz, yb.reshape(T, W_ATT), p3,
                        cln_g[i].reshape(1, W_CONV), cln_b[i].reshape(1, W_CONV),
                        wo, wpe, pe_g[i].reshape(1, D_MODEL), wpg, ng.reshape(1, D_MODEL),
                        w_in, w_scale, i, last=last)
        if last:
            (h,) = outs
        else:
            h, hn, win = outs
    return h.reshape(B, S, D_MODEL)
```

```python
import functools

import numpy as np
import jax
import jax.numpy as jnp
from jax import lax
from jax.experimental import pallas as pl
from jax.experimental.pallas import tpu as pltpu

D_MODEL = 2048
DEPTH = 4
W_CONV = 1024
HEAD_DIM = 64
N_Q_HEADS = 16
N_KV_HEADS = 4
GQA_GROUP = 4
W_ATT = 1024
W_KV = 256
CONV_WIDTH = 31
CONV_PAD = 15
WINDOW = 128
BLK = 128
NUM_BUCKETS = 32
MAX_DISTANCE = 128
PLE_DIM = 256
EPS = 1e-6
NEG = -1e30
LOG2E = float(np.log2(np.e))

LANES = 128
SUBLANES = 8
HALO = 16
N_SLAB = W_CONV // LANES
HW = GQA_GROUP * HEAD_DIM
MXU_N = 256
N_GROUP = 2
GW = W_ATT // N_GROUP
KVW = (2 * W_KV) // N_GROUP
assert KVW % MXU_N == 0 and GW % MXU_N == 0
VMEM_LIMIT = 62 * 1024 * 1024

F32 = jnp.float32
BF16 = jnp.bfloat16


def _sigmoid(x):
    return jax.nn.sigmoid(x)


def _silu(x):
    return x * jax.nn.sigmoid(x)


def _rmsnorm(x, g):
    ms = jnp.mean(x * x, axis=-1, keepdims=True)
    return (x * lax.rsqrt(ms + EPS)) * g


def _inproj_kernel(x_ref, g_ref, wav_ref, wag_ref, waz_ref, wbz_ref, wq_ref, wk_ref, wv_ref,
                   wo32_ref, wpg32_ref, a_ref, gz_ref, gb_ref, q_ref, kv_ref, wo_ref, wpg_ref,
                   *scratch, fuse_norm):
    wo_ref[...] = wo32_ref[...].astype(BF16)
    wpg_ref[...] = wpg32_ref[...].astype(BF16)
    if fuse_norm:
        (hn_ref,) = scratch

        @pl.when(pl.program_id(1) == 0)
        def _():
            hn_ref[...] = _rmsnorm(x_ref[...], g_ref[...]).astype(BF16)

        hn = hn_ref[...]
    else:
        hn = x_ref[...]

    def proj(w):
        return jnp.dot(hn, w, preferred_element_type=F32)

    a = proj(wav_ref[...]) * _sigmoid(proj(wag_ref[...]))
    for s in range(GW // LANES):
        a_ref[s] = a[:, s * LANES:(s + 1) * LANES]
    gz_ref[...] = _silu(proj(waz_ref[...]))
    gb_ref[...] = _silu(proj(wbz_ref[...]))
    q_ref[...] = proj(wq_ref[...]).astype(BF16)
    kv = proj(jnp.concatenate([wk_ref[...], wv_ref[...]], axis=1))
    k2, v2 = kv[:, :LANES], kv[:, LANES:]
    k2r, v2r = pltpu.roll(k2, HEAD_DIM, 1), pltpu.roll(v2, HEAD_DIM, 1)
    lo = lax.broadcasted_iota(jnp.int32, k2.shape, 1) < HEAD_DIM
    pieces = [jnp.where(lo, k2, v2r), jnp.where(lo, v2, k2r),
              jnp.where(lo, k2r, v2), jnp.where(lo, v2r, k2)]
    for s, piece in enumerate(pieces):
        kv_ref[:, s * LANES:(s + 1) * LANES] = piece.astype(BF16)


def _inproj(x, g, w_in, w_out, w_pg, layer, *, fuse_norm, tm=None):
    T = x.shape[0]
    if tm is None:
        tm = 512 if fuse_norm else 1024
    scratch = [pltpu.VMEM((tm, D_MODEL), BF16)] if fuse_norm else []
    steps = (T // tm) * N_GROUP
    wrows = D_MODEL // steps
    assert wrows * steps == D_MODEL and wrows % 16 == 0

    def col(dtype, width=GW):
        return (pl.BlockSpec((tm, width), lambda i, j: (i, j)),
                jax.ShapeDtypeStruct((T, N_GROUP * width), dtype))

    def wcols(offset, width):
        base = offset // width
        assert base * width == offset
        return pl.BlockSpec((D_MODEL, width), lambda i, j: (0, base + j))

    cast_in = pl.BlockSpec((None, wrows, D_MODEL), lambda i, j: (layer, i * N_GROUP + j, 0))
    cast_out = (pl.BlockSpec((wrows, D_MODEL), lambda i, j: (i * N_GROUP + j, 0)),
                jax.ShapeDtypeStruct((D_MODEL, D_MODEL), BF16))

    o = np.cumsum([0, W_CONV, W_CONV, W_CONV, W_ATT, W_KV, W_KV])
    kw = W_KV // N_GROUP
    w_specs = [wcols(o[0], GW), wcols(o[1], GW), wcols(o[2], GW), wcols(o[6], GW),
               wcols(o[3], GW), wcols(o[4], kw), wcols(o[5], kw)]
    specs, shapes = zip(col(F32), col(F32), col(BF16), col(BF16, 2 * KVW), cast_out, cast_out)
    spg = GW // LANES
    return pl.pallas_call(
        functools.partial(_inproj_kernel, fuse_norm=fuse_norm),
        grid=(T // tm, N_GROUP),
        in_specs=[
            pl.BlockSpec((tm, D_MODEL), lambda i, j: (i, 0)),
            pl.BlockSpec((1, D_MODEL), lambda i, j: (0, 0)),
        ] + w_specs + [cast_in, cast_in],
        out_specs=[pl.BlockSpec((spg, tm, LANES), lambda i, j: (j, i, 0))] + list(specs),
        out_shape=[jax.ShapeDtypeStruct((N_SLAB, T, LANES), F32)] + list(shapes),
        scratch_shapes=scratch,
        compiler_params=pltpu.CompilerParams(
            dimension_semantics=("arbitrary", "arbitrary"),
            vmem_limit_bytes=VMEM_LIMIT),
        name="inproj_norm" if fuse_norm else "inproj",
    )(x, g, *([w_in] * len(w_specs)), w_out, w_pg)


def _conv_kernel(ap_ref, ac_ref, an_ref, cw_ref, cb_ref, o_ref, a_ext, *, ts, rc):
    i = pl.program_id(1)
    n = pl.num_programs(1)
    zero = jnp.zeros((N_SLAB, HALO, LANES), F32)
    a_ext[:, 0:HALO, :] = jnp.where(i > 0, ap_ref[:, 0], zero)
    a_ext[:, HALO:HALO + ts, :] = ac_ref[:, 0]
    a_ext[:, HALO + ts:, :] = jnp.where(i < n - 1, an_ref[:, 0], zero)

    n_sub = rc // SUBLANES
    first = HALO - CONV_PAD
    n_off = CONV_WIDTH + (n_sub - 1) * SUBLANES

    for c in range(N_SLAB):
        taps = [jnp.broadcast_to(cw_ref[c, k:k + 1, :], (SUBLANES, LANES))
                for k in range(CONV_WIDTH)]
        bias = jnp.broadcast_to(cb_ref[c], (SUBLANES, LANES))

        def conv_chunk(r, carry, c=c, taps=taps, bias=bias):
            r0 = pl.multiple_of(r * rc, rc)
            accs = [[bias, None] for _ in range(n_sub)]
            for o in range(n_off):
                win = a_ext[c, pl.ds(r0 + first + o, SUBLANES, stride=1), :]
                for j in range(n_sub):
                    k = o - j * SUBLANES
                    if 0 <= k < CONV_WIDTH:
                        prod = win * taps[k]
                        cur = accs[j][k % 2]
                        accs[j][k % 2] = prod if cur is None else cur + prod
            o_ref[c, 0, pl.ds(r0, rc), :] = jnp.concatenate([e + o_ for e, o_ in accs], axis=0)
            return carry

        lax.fori_loop(0, ts // rc, conv_chunk, 0)


def _conv_branch(a4, cw, cb, *, ts=1024, rc=128):
    _, B, S, _ = a4.shape
    nh = ts // HALO
    last_h = S // HALO - 1

    def const(shape):
        return pl.BlockSpec(shape, lambda b, i: (0,) * len(shape))

    return pl.pallas_call(
        functools.partial(_conv_kernel, ts=ts, rc=rc),
        grid=(B, S // ts),
        in_specs=[
            pl.BlockSpec((N_SLAB, 1, HALO, LANES),
                         lambda b, i: (0, b, jnp.maximum(i * nh - 1, 0), 0)),
            pl.BlockSpec((N_SLAB, 1, ts, LANES), lambda b, i: (0, b, i, 0)),
            pl.BlockSpec((N_SLAB, 1, HALO, LANES),
                         lambda b, i: (0, b, jnp.minimum((i + 1) * nh, last_h), 0)),
            const((N_SLAB, CONV_WIDTH, LANES)), const((N_SLAB, 1, LANES))],
        out_specs=pl.BlockSpec((N_SLAB, 1, ts, LANES), lambda b, i: (0, b, i, 0)),
        out_shape=jax.ShapeDtypeStruct((N_SLAB, B, S, LANES), F32),
        scratch_shapes=[pltpu.VMEM((N_SLAB, ts + 2 * HALO, LANES), F32)],
        compiler_params=pltpu.CompilerParams(
            dimension_semantics=("arbitrary", "arbitrary"),
            vmem_limit_bytes=VMEM_LIMIT),
        name="conv_branch",
    )(a4, a4, a4, cw, cb)


def _attn_kernel(sink_ref, q_ref, kvp_ref, kvc_ref, kvn_ref, gb_ref, bias_ref, o_ref, *, qb):
    step = pl.program_id(1)
    last_blk = pl.num_programs(1) * qb - 1
    nk = 3 * BLK
    lo = lax.broadcasted_iota(jnp.int32, (nk, LANES), 1) < HEAD_DIM
    lo_q = lax.broadcasted_iota(jnp.int32, (2 * BLK, LANES), 1) < HEAD_DIM
    top = lax.broadcasted_iota(jnp.int32, (2 * BLK, 1), 0) < BLK
    zeros = jnp.zeros((nk, LANES), BF16)
    ones_lo = jnp.where(lo, 1.0, 0.0).astype(BF16)
    ones_hi = jnp.where(lo, 0.0, 1.0).astype(BF16)
    sum_cols = jnp.concatenate([ones_lo, ones_hi], axis=0)

    def keys(sub, cols):
        parts = []
        for t in (sub - 1, sub, sub + 1):
            if t < 0:
                parts.append(kvp_ref[0, :, cols])
            elif t >= qb:
                parts.append(kvn_ref[0, :, cols])
            else:
                parts.append(kvc_ref[0, t * BLK:(t + 1) * BLK, cols])
        return jnp.concatenate(parts, axis=0)

    for sub, h in [(sub, h) for sub in range(qb) for h in range(N_KV_HEADS)]:
        blk = step * qb + sub
        variant = jnp.where(blk == 0, 0, jnp.where(blk == last_blk, 2, 1))
        rows = slice(sub * BLK, (sub + 1) * BLK)
        kv = keys(sub, slice(h * HW, h * HW + LANES))
        vk = keys(sub, slice(h * HW + LANES, (h + 1) * HW))
        kblk = jnp.concatenate([jnp.where(lo, kv, zeros), jnp.where(lo, zeros, vk)], axis=0)
        vblk = jnp.concatenate([jnp.where(lo, vk, zeros), jnp.where(lo, zeros, kv)], axis=0)
        vblk = jnp.concatenate([vblk, sum_cols], axis=1)
        hs = slice(h * HW, (h + 1) * HW)
        qh = q_ref[0, rows, hs]
        lhs = jnp.concatenate([qh[:, :LANES], qh[:, LANES:]], axis=0)
        s = lax.dot_general(lhs, kblk, (((1,), (1,)), ((), ())), preferred_element_type=F32)
        s = s + bias_ref[variant, h]
        sa, sb = s[:, :nk], s[:, nk:]
        g0 = h * GQA_GROUP
        sk_a = jnp.where(top, sink_ref[g0], sink_ref[g0 + 2])
        sk_b = jnp.where(top, sink_ref[g0 + 1], sink_ref[g0 + 3])
        ma = jnp.maximum(jnp.max(sa, axis=-1, keepdims=True), sk_a)
        mb = jnp.maximum(jnp.max(sb, axis=-1, keepdims=True), sk_b)
        pe = jnp.concatenate([jnp.exp2(sa - ma), jnp.exp2(sb - mb)], axis=1).astype(BF16)
        o = jnp.dot(pe, vblk, preferred_element_type=F32)
        den = o[:, LANES:] + jnp.where(lo_q, jnp.exp2(sk_a - ma), jnp.exp2(sk_b - mb))
        y = o[:, :LANES] * (1.0 / den)
        yh = jnp.concatenate([y[:BLK], y[BLK:]], axis=1)
        o_ref[0, rows, hs] = (yh * gb_ref[0, rows, hs]).astype(BF16)


def _attn_branch(sink, q3, kv3, gb3, bias4, *, qb=8):
    B, S, _ = q3.shape
    nb = S // BLK
    kvw = kv3.shape[-1]

    def row(width=W_ATT):
        return pl.BlockSpec((1, qb * BLK, width), lambda b, j: (b, j, 0))

    def halo(d):
        return pl.BlockSpec((1, BLK, kvw),
                            lambda b, j: (b, jnp.clip(j * qb + d, 0, nb - 1), 0))

    return pl.pallas_call(
        functools.partial(_attn_kernel, qb=qb),
        grid=(B, nb // qb),
        in_specs=[
            pl.BlockSpec(memory_space=pltpu.SMEM),
            row(), halo(-1), row(kvw), halo(qb), row(),
            pl.BlockSpec(bias4.shape, lambda b, j: (0, 0, 0, 0), pipeline_mode=pl.Buffered(1)),
        ],
        out_specs=row(),
        out_shape=jax.ShapeDtypeStruct((B, S, W_ATT), BF16),
        compiler_params=pltpu.CompilerParams(
            dimension_semantics=("arbitrary", "arbitrary"),
            vmem_limit_bytes=VMEM_LIMIT),
        name="attn_branch",
    )(sink, q3, kv3, kv3, kv3, gb3, bias4)


def _outproj_kernel(h_ref, yc_ref, gz_ref, yb_ref, p_ref, lg_ref, lb_ref, woa_ref, wob_ref,
                    wpe_ref, peg_ref, wpg_ref, ng_ref, *rest, last):
    if last:
        (o_ref,) = rest
    else:
        win32_ref, scale_ref, h_out, hn_out, win_ref = rest
        win_ref[...] = (win32_ref[...] * scale_ref[...]).astype(BF16)
    ys = [yc_ref[c] for c in range(N_SLAB)]
    tot = ys[0]
    for c in range(1, N_SLAB):
        tot = tot + ys[c]
    mu = jnp.sum(tot, axis=-1, keepdims=True) * (1.0 / W_CONV)
    xcs = [y - mu for y in ys]
    sq = xcs[0] * xcs[0]
    for c in range(1, N_SLAB):
        sq = sq + xcs[c] * xcs[c]
    var = jnp.sum(sq, axis=-1, keepdims=True) * (1.0 / W_CONV)
    rs = lax.rsqrt(var + EPS)
    ya = []
    for c in range(N_SLAB):
        cs = slice(c * LANES, (c + 1) * LANES)
        y = (xcs[c] * rs) * lg_ref[:, cs] + lb_ref[:, cs]
        ya.append((_silu(y) * gz_ref[:, cs]).astype(BF16))
    ya = jnp.concatenate(ya, axis=1)

    h1 = h_ref[...] + jnp.dot(yb_ref[...], wob_ref[...], preferred_element_type=F32)
    h1 = h1 + jnp.dot(ya, woa_ref[...], preferred_element_type=F32)
    e = jnp.dot(p_ref[...].astype(BF16), wpe_ref[...], preferred_element_type=F32)
    e = _rmsnorm(e, peg_ref[...])
    gate = jnp.dot(h1.astype(BF16), wpg_ref[...], preferred_element_type=F32)
    h2 = h1 + e * _sigmoid(gate)
    if last:
        o_ref[...] = _rmsnorm(h2, ng_ref[...])
    else:
        h_out[...] = h2
        hn_out[...] = _rmsnorm(h2, ng_ref[...]).astype(BF16)


def _outproj(h, yc, gz, yb, p, lg, lb, wo, wpe, peg, wpg, ng, w_in32, w_scale, layer, *, last,
             tm=512):
    T = h.shape[0]
    steps = T // tm
    wrows = D_MODEL // steps
    assert wrows * steps == D_MODEL and wrows % 16 == 0
    w_in_cols = w_in32.shape[-1]

    def row(width):
        return pl.BlockSpec((tm, width), lambda i: (i, 0))

    def const(shape, blk=0):
        return pl.BlockSpec(shape, lambda i: (blk, 0), pipeline_mode=pl.Buffered(1))

    in_specs = [row(D_MODEL),
                pl.BlockSpec((N_SLAB, tm, LANES), lambda i: (0, i, 0)),
                row(W_CONV), row(W_ATT),
                pl.BlockSpec((None, tm, PLE_DIM), lambda i: (layer, i, 0)),
                const((1, W_CONV)), const((1, W_CONV)),
                const((W_CONV, D_MODEL), 0), const((W_ATT, D_MODEL), 1),
                pl.BlockSpec((None, PLE_DIM, D_MODEL), lambda i: (layer, 0, 0),
                             pipeline_mode=pl.Buffered(1)),
                const((1, D_MODEL)), const((D_MODEL, D_MODEL)), const((1, D_MODEL))]
    args = [h, yc, gz, yb, p, lg, lb, wo, wo, wpe, peg, wpg, ng]
    if last:
        out_specs = [row(D_MODEL)]
        out_shape = [jax.ShapeDtypeStruct((T, D_MODEL), F32)]
    else:
        in_specs += [pl.BlockSpec((None, wrows, w_in_cols), lambda i: (layer + 1, i, 0)),
                     const((1, w_in_cols))]
        args += [w_in32, w_scale]
        out_specs = [row(D_MODEL), row(D_MODEL),
                     pl.BlockSpec((wrows, w_in_cols), lambda i: (i, 0))]
        out_shape = [jax.ShapeDtypeStruct((T, D_MODEL), F32),
                     jax.ShapeDtypeStruct((T, D_MODEL), BF16),
                     jax.ShapeDtypeStruct((D_MODEL, w_in_cols), BF16)]
    return pl.pallas_call(
        functools.partial(_outproj_kernel, last=last),
        grid=(steps,),
        in_specs=in_specs,
        out_specs=out_specs,
        out_shape=out_shape,
        compiler_params=pltpu.CompilerParams(
            dimension_semantics=("arbitrary",),
            vmem_limit_bytes=VMEM_LIMIT),
        name="outproj_final" if last else "outproj",
    )(*args)


def _band_buckets():
    q_off = np.arange(BLK)[:, None]
    k_off = np.arange(3 * BLK)[None, :] - BLK
    rel = k_off - q_off
    half = NUM_BUCKETS // 2
    ret = (rel > 0).astype(np.int32) * half
    n = np.abs(rel)
    max_exact = half // 2
    large = max_exact + (np.log(np.maximum(n, 1) / max_exact)
                         / np.log(MAX_DISTANCE / max_exact)
                         * (half - max_exact)).astype(np.int32)
    large = np.minimum(large, half - 1)
    ret = ret + np.where(n < max_exact, n, large)
    return ret.astype(np.int32), (n <= WINDOW)


def _bias_table(rel_bias):
    buckets, band = _band_buckets()
    onehot = np.zeros((BLK * 3 * BLK, NUM_BUCKETS), np.float32)
    onehot[np.arange(onehot.shape[0]), buckets.reshape(-1)] = 1.0
    bias = jnp.dot(rel_bias.astype(F32).T, jnp.asarray(onehot.T),
                   precision=lax.Precision.HIGHEST)
    bias = bias.reshape(N_Q_HEADS, BLK, 3 * BLK)
    bias = jnp.where(jnp.asarray(band)[None], bias * LOG2E, NEG)
    col = np.arange(3 * BLK)
    first = jnp.where(jnp.asarray(col < BLK)[None, None], NEG, bias)
    last = jnp.where(jnp.asarray(col >= 2 * BLK)[None, None], NEG, bias)
    tab = jnp.stack([first, bias, last])
    tab = tab.reshape(3, N_Q_HEADS // 2, 2, BLK, 3 * BLK)
    tab = jnp.concatenate([tab[:, :, 0], tab[:, :, 1]], axis=-1)
    return tab.reshape(3, N_KV_HEADS, 2 * BLK, 6 * BLK)


def _w_in_scale():
    q0 = 3 * W_CONV
    col = np.arange(3 * W_CONV + 2 * W_ATT + 2 * W_KV)
    scale = np.where((col >= q0) & (col < q0 + W_ATT), HEAD_DIM ** -0.5 * LOG2E, 1.0)
    return jnp.asarray(scale, F32).reshape(1, -1)


def kernel(x, p, norm_g, w_in, conv_w, conv_b, cln_g, cln_b, sink, rel_bias,
           w_out, w_pe, pe_g, w_pg, final_g):
    B, S, _ = x.shape
    T = B * S
    bias4 = _bias_table(rel_bias)
    w_scale = _w_in_scale()
    win = (w_in[0] * w_scale).astype(BF16)
    wpe = w_pe.astype(BF16)
    cw = jnp.transpose(conv_w.reshape(DEPTH, CONV_WIDTH, N_SLAB, LANES), (0, 2, 1, 3))
    p3 = p.reshape(DEPTH, T, PLE_DIM)
    h = x.reshape(T, D_MODEL)
    hn = h
    for i in range(DEPTH):
        a, gz, gb, q, kv, wo, wpg = _inproj(hn, norm_g[i].reshape(1, D_MODEL), win, w_out, w_pg,
                                            i, fuse_norm=(i == 0))
        yc = _conv_branch(a.reshape(N_SLAB, B, S, LANES), cw[i],
                          conv_b[i].reshape(N_SLAB, 1, LANES))
        yb = _attn_branch(sink[i] * LOG2E, q.reshape(B, S, W_ATT),
                          kv.reshape(B, S, N_KV_HEADS * HW), gb.reshape(B, S, W_ATT), bias4)
        last = i == DEPTH - 1
        ng = final_g if last else norm_g[i + 1]
        outs = _outproj(h, yc.reshape(N_SLAB, T, LANES), gz, yb.reshape(T, W_ATT), p3,
                        cln_g[i].reshape(1, W_CONV), cln_b[i].reshape(1, W_CONV),
                        wo, wpe, pe_g[i].reshape(1, D_MODEL), wpg, ng.reshape(1, D_MODEL),
                        w_in, w_scale, i, last=last)
        if last:
            (h,) = outs
        else:
            h, hn, win = outs
    return h.reshape(B, S, D_MODEL)
```

```python
import functools

import numpy as np
import jax
import jax.numpy as jnp
from jax import lax
from jax.experimental import pallas as pl
from jax.experimental.pallas import tpu as pltpu

D_MODEL = 2048
DEPTH = 4
W_CONV = 1024
HEAD_DIM = 64
N_Q_HEADS = 16
N_KV_HEADS = 4
GQA_GROUP = 4
W_ATT = 1024
W_KV = 256
CONV_WIDTH = 31
CONV_PAD = 15
WINDOW = 128
BLK = 128
NUM_BUCKETS = 32
MAX_DISTANCE = 128
PLE_DIM = 256
EPS = 1e-6
NEG = -1e30
LOG2E = float(np.log2(np.e))

LANES = 128
SUBLANES = 8
HALO = 16
N_SLAB = W_CONV // LANES
HW = GQA_GROUP * HEAD_DIM
MXU_N = 256
N_GROUP = 2
GW = W_ATT // N_GROUP
KVW = (2 * W_KV) // N_GROUP
assert KVW % MXU_N == 0 and GW % MXU_N == 0
VMEM_LIMIT = 62 * 1024 * 1024

F32 = jnp.float32
BF16 = jnp.bfloat16


def _sigmoid(x):
    return jax.nn.sigmoid(x)


def _silu(x):
    return x * jax.nn.sigmoid(x)


def _rmsnorm(x, g):
    ms = jnp.mean(x * x, axis=-1, keepdims=True)
    return (x * lax.rsqrt(ms + EPS)) * g


def _inproj_kernel(x_ref, g_ref, wav_ref, wag_ref, waz_ref, wbz_ref, wq_ref, wk_ref, wv_ref,
                   wo32_ref, wpg32_ref, a_ref, gz_ref, gb_ref, q_ref, kv_ref, wo_ref, wpg_ref,
                   *, fuse_norm):
    wo_ref[...] = wo32_ref[...].astype(BF16)
    wpg_ref[...] = wpg32_ref[...].astype(BF16)
    if fuse_norm:
        hn = _rmsnorm(x_ref[...], g_ref[...]).astype(BF16)
    else:
        hn = x_ref[...]

    def proj(w):
        return jnp.dot(hn, w, preferred_element_type=F32)

    a = proj(wav_ref[...]) * _sigmoid(proj(wag_ref[...]))
    for s in range(GW // LANES):
        a_ref[s] = a[:, s * LANES:(s + 1) * LANES]
    gz_ref[...] = _silu(proj(waz_ref[...]))
    gb_ref[...] = _silu(proj(wbz_ref[...]))
    q_ref[...] = proj(wq_ref[...]).astype(BF16)
    kv = proj(jnp.concatenate([wk_ref[...], wv_ref[...]], axis=1))
    k2, v2 = kv[:, :LANES], kv[:, LANES:]
    k2r, v2r = pltpu.roll(k2, HEAD_DIM, 1), pltpu.roll(v2, HEAD_DIM, 1)
    lo = lax.broadcasted_iota(jnp.int32, k2.shape, 1) < HEAD_DIM
    pieces = [jnp.where(lo, k2, v2r), jnp.where(lo, v2, k2r),
              jnp.where(lo, k2r, v2), jnp.where(lo, v2r, k2)]
    for s, piece in enumerate(pieces):
        kv_ref[:, s * LANES:(s + 1) * LANES] = piece.astype(BF16)


def _inproj(x, g, w_in, w_out, w_pg, layer, *, fuse_norm, tm=1024):
    T = x.shape[0]
    steps = (T // tm) * N_GROUP
    wrows = D_MODEL // steps
    assert wrows * steps == D_MODEL and wrows % 16 == 0

    def col(dtype, width=GW):
        return (pl.BlockSpec((tm, width), lambda i, j: (i, j)),
                jax.ShapeDtypeStruct((T, N_GROUP * width), dtype))

    def wcols(offset, width):
        base = offset // width
        assert base * width == offset
        return pl.BlockSpec((D_MODEL, width), lambda i, j: (0, base + j))

    cast_in = pl.BlockSpec((None, wrows, D_MODEL), lambda i, j: (layer, i * N_GROUP + j, 0))
    cast_out = (pl.BlockSpec((wrows, D_MODEL), lambda i, j: (i * N_GROUP + j, 0)),
                jax.ShapeDtypeStruct((D_MODEL, D_MODEL), BF16))

    o = np.cumsum([0, W_CONV, W_CONV, W_CONV, W_ATT, W_KV, W_KV])
    kw = W_KV // N_GROUP
    w_specs = [wcols(o[0], GW), wcols(o[1], GW), wcols(o[2], GW), wcols(o[6], GW),
               wcols(o[3], GW), wcols(o[4], kw), wcols(o[5], kw)]
    specs, shapes = zip(col(F32), col(F32), col(BF16), col(BF16, 2 * KVW), cast_out, cast_out)
    spg = GW // LANES
    return pl.pallas_call(
        functools.partial(_inproj_kernel, fuse_norm=fuse_norm),
        grid=(T // tm, N_GROUP),
        in_specs=[
            pl.BlockSpec((tm, D_MODEL), lambda i, j: (i, 0)),
            pl.BlockSpec((1, D_MODEL), lambda i, j: (0, 0)),
        ] + w_specs + [cast_in, cast_in],
        out_specs=[pl.BlockSpec((spg, tm, LANES), lambda i, j: (j, i, 0))] + list(specs),
        out_shape=[jax.ShapeDtypeStruct((N_SLAB, T, LANES), F32)] + list(shapes),
        compiler_params=pltpu.CompilerParams(
            dimension_semantics=("arbitrary", "arbitrary"),
            vmem_limit_bytes=VMEM_LIMIT),
        name="inproj_norm" if fuse_norm else "inproj",
    )(x, g, *([w_in] * len(w_specs)), w_out, w_pg)


def _conv_kernel(ap_ref, ac_ref, an_ref, cw_ref, o_ref, a_ext, *, ts, rc):
    i = pl.program_id(1)
    n = pl.num_programs(1)
    zero = jnp.zeros((N_SLAB, HALO, LANES), F32)
    a_ext[:, 0:HALO, :] = jnp.where(i > 0, ap_ref[:, 0], zero)
    a_ext[:, HALO:HALO + ts, :] = ac_ref[:, 0]
    a_ext[:, HALO + ts:, :] = jnp.where(i < n - 1, an_ref[:, 0], zero)

    n_sub = rc // SUBLANES
    first = HALO - CONV_PAD
    n_off = CONV_WIDTH + (n_sub - 1) * SUBLANES

    for c in range(N_SLAB):
        taps = [jnp.broadcast_to(cw_ref[c, k:k + 1, :], (SUBLANES, LANES))
                for k in range(CONV_WIDTH)]

        def conv_chunk(r, carry, c=c, taps=taps):
            r0 = pl.multiple_of(r * rc, rc)
            accs = [[None, None] for _ in range(n_sub)]
            for o in range(n_off):
                win = a_ext[c, pl.ds(r0 + first + o, SUBLANES, stride=1), :]
                for j in range(n_sub):
                    k = o - j * SUBLANES
                    if 0 <= k < CONV_WIDTH:
                        prod = win * taps[k]
                        cur = accs[j][k % 2]
                        accs[j][k % 2] = prod if cur is None else cur + prod
            o_ref[c, 0, pl.ds(r0, rc), :] = jnp.concatenate([e + o_ for e, o_ in accs], axis=0)
            return carry

        lax.fori_loop(0, ts // rc, conv_chunk, 0)


def _conv_branch(a4, cw, *, ts=1024, rc=128):
    _, B, S, _ = a4.shape
    nh = ts // HALO
    last_h = S // HALO - 1

    def const(shape):
        return pl.BlockSpec(shape, lambda b, i: (0,) * len(shape))

    return pl.pallas_call(
        functools.partial(_conv_kernel, ts=ts, rc=rc),
        grid=(B, S // ts),
        in_specs=[
            pl.BlockSpec((N_SLAB, 1, HALO, LANES),
                         lambda b, i: (0, b, jnp.maximum(i * nh - 1, 0), 0)),
            pl.BlockSpec((N_SLAB, 1, ts, LANES), lambda b, i: (0, b, i, 0)),
            pl.BlockSpec((N_SLAB, 1, HALO, LANES),
                         lambda b, i: (0, b, jnp.minimum((i + 1) * nh, last_h), 0)),
            const((N_SLAB, CONV_WIDTH, LANES))],
        out_specs=pl.BlockSpec((N_SLAB, 1, ts, LANES), lambda b, i: (0, b, i, 0)),
        out_shape=jax.ShapeDtypeStruct((N_SLAB, B, S, LANES), F32),
        scratch_shapes=[pltpu.VMEM((N_SLAB, ts + 2 * HALO, LANES), F32)],
        compiler_params=pltpu.CompilerParams(
            dimension_semantics=("arbitrary", "arbitrary"),
            vmem_limit_bytes=VMEM_LIMIT),
        name="conv_branch",
    )(a4, a4, a4, cw)


def _attn_kernel(sink_ref, q_ref, kvp_ref, kvc_ref, kvn_ref, gb_ref, bias_ref, o_ref, *, qb):
    step = pl.program_id(1)
    last_blk = pl.num_programs(1) * qb - 1
    nk = 3 * BLK
    lo = lax.broadcasted_iota(jnp.int32, (nk, LANES), 1) < HEAD_DIM
    lo_q = lax.broadcasted_iota(jnp.int32, (2 * BLK, LANES), 1) < HEAD_DIM
    top = lax.broadcasted_iota(jnp.int32, (2 * BLK, 1), 0) < BLK
    zeros = jnp.zeros((nk, LANES), BF16)
    ones_lo = jnp.where(lo, 1.0, 0.0).astype(BF16)
    ones_hi = jnp.where(lo, 0.0, 1.0).astype(BF16)
    sum_cols = jnp.concatenate([ones_lo, ones_hi], axis=0)

    def keys(sub, cols):
        parts = []
        for t in (sub - 1, sub, sub + 1):
            if t < 0:
                parts.append(kvp_ref[0, :, cols])
            elif t >= qb:
                parts.append(kvn_ref[0, :, cols])
            else:
                parts.append(kvc_ref[0, t * BLK:(t + 1) * BLK, cols])
        return jnp.concatenate(parts, axis=0)

    for sub, h in [(sub, h) for sub in range(qb) for h in range(N_KV_HEADS)]:
        blk = step * qb + sub
        variant = jnp.where(blk == 0, 0, jnp.where(blk == last_blk, 2, 1))
        rows = slice(sub * BLK, (sub + 1) * BLK)
        kv = keys(sub, slice(h * HW, h * HW + LANES))
        vk = keys(sub, slice(h * HW + LANES, (h + 1) * HW))
        kblk = jnp.concatenate([jnp.where(lo, kv, zeros), jnp.where(lo, zeros, vk)], axis=0)
        vblk = jnp.concatenate([jnp.where(lo, vk, zeros), jnp.where(lo, zeros, kv)], axis=0)
        vblk = jnp.concatenate([vblk, sum_cols], axis=1)
        hs = slice(h * HW, (h + 1) * HW)
        qh = q_ref[0, rows, hs]
        lhs = jnp.concatenate([qh[:, :LANES], qh[:, LANES:]], axis=0)
        s = lax.dot_general(lhs, kblk, (((1,), (1,)), ((), ())), preferred_element_type=F32)
        s = s + bias_ref[variant, h]
        sa, sb = s[:, :nk], s[:, nk:]
        g0 = h * GQA_GROUP
        sk_a = jnp.where(top, sink_ref[g0], sink_ref[g0 + 2])
        sk_b = jnp.where(top, sink_ref[g0 + 1], sink_ref[g0 + 3])
        ma = jnp.maximum(jnp.max(sa, axis=-1, keepdims=True), sk_a)
        mb = jnp.maximum(jnp.max(sb, axis=-1, keepdims=True), sk_b)
        pe = jnp.concatenate([jnp.exp2(sa - ma), jnp.exp2(sb - mb)], axis=1).astype(BF16)
        o = jnp.dot(pe, vblk, preferred_element_type=F32)
        den = o[:, LANES:] + jnp.where(lo_q, jnp.exp2(sk_a - ma), jnp.exp2(sk_b - mb))
        y = o[:, :LANES] * (1.0 / den)
        yh = jnp.concatenate([y[:BLK], y[BLK:]], axis=1)
        o_ref[0, rows, hs] = (yh * gb_ref[0, rows, hs]).astype(BF16)


def _attn_branch(sink, q3, kv3, gb3, bias4, *, qb=8):
    B, S, _ = q3.shape
    nb = S // BLK
    kvw = kv3.shape[-1]

    def row(width=W_ATT):
        return pl.BlockSpec((1, qb * BLK, width), lambda b, j: (b, j, 0))

    def halo(d):
        return pl.BlockSpec((1, BLK, kvw),
                            lambda b, j: (b, jnp.clip(j * qb + d, 0, nb - 1), 0))

    return pl.pallas_call(
        functools.partial(_attn_kernel, qb=qb),
        grid=(B, nb // qb),
        in_specs=[
            pl.BlockSpec(memory_space=pltpu.SMEM),
            row(), halo(-1), row(kvw), halo(qb), row(),
            pl.BlockSpec(bias4.shape, lambda b, j: (0, 0, 0, 0), pipeline_mode=pl.Buffered(1)),
        ],
        out_specs=row(),
        out_shape=jax.ShapeDtypeStruct((B, S, W_ATT), BF16),
        compiler_params=pltpu.CompilerParams(
            dimension_semantics=("arbitrary", "arbitrary"),
            vmem_limit_bytes=VMEM_LIMIT),
        name="attn_branch",
    )(sink, q3, kv3, kv3, kv3, gb3, bias4)


def _outproj_kernel(h_ref, yc_ref, cb_ref, gz_ref, yb_ref, p_ref, lg_ref, lb_ref, woa_ref,
                    wob_ref, wpe_ref, peg_ref, wpg_ref, ng_ref, *rest, last):
    if last:
        (o_ref,) = rest
    else:
        win32_ref, scale_ref, h_out, hn_out, win_ref = rest
        win_ref[...] = (win32_ref[...] * scale_ref[...]).astype(BF16)
    ys = [yc_ref[c] + cb_ref[c] for c in range(N_SLAB)]
    tot = ys[0]
    for c in range(1, N_SLAB):
        tot = tot + ys[c]
    mu = jnp.sum(tot, axis=-1, keepdims=True) * (1.0 / W_CONV)
    xcs = [y - mu for y in ys]
    sq = xcs[0] * xcs[0]
    for c in range(1, N_SLAB):
        sq = sq + xcs[c] * xcs[c]
    var = jnp.sum(sq, axis=-1, keepdims=True) * (1.0 / W_CONV)
    rs = lax.rsqrt(var + EPS)
    ya = []
    for c in range(N_SLAB):
        cs = slice(c * LANES, (c + 1) * LANES)
        y = (xcs[c] * rs) * lg_ref[:, cs] + lb_ref[:, cs]
        ya.append((_silu(y) * gz_ref[:, cs]).astype(BF16))
    ya = jnp.concatenate(ya, axis=1)

    h1 = h_ref[...] + jnp.dot(yb_ref[...], wob_ref[...], preferred_element_type=F32)
    h1 = h1 + jnp.dot(ya, woa_ref[...], preferred_element_type=F32)
    e = jnp.dot(p_ref[...].astype(BF16), wpe_ref[...], preferred_element_type=F32)
    e = _rmsnorm(e, peg_ref[...])
    gate = jnp.dot(h1.astype(BF16), wpg_ref[...], preferred_element_type=F32)
    h2 = h1 + e * _sigmoid(gate)
    if last:
        o_ref[...] = _rmsnorm(h2, ng_ref[...])
    else:
        h_out[...] = h2
        hn_out[...] = _rmsnorm(h2, ng_ref[...]).astype(BF16)


def _outproj(h, yc, cb, gz, yb, p, lg, lb, wo, wpe, peg, wpg, ng, w_in32, w_scale, layer, *,
             last, tm=512):
    T = h.shape[0]
    steps = T // tm
    wrows = D_MODEL // steps
    assert wrows * steps == D_MODEL and wrows % 16 == 0
    w_in_cols = w_in32.shape[-1]

    def row(width):
        return pl.BlockSpec((tm, width), lambda i: (i, 0))

    def const(shape, blk=0):
        return pl.BlockSpec(shape, lambda i: (blk, 0), pipeline_mode=pl.Buffered(1))

    in_specs = [row(D_MODEL),
                pl.BlockSpec((N_SLAB, tm, LANES), lambda i: (0, i, 0)),
                pl.BlockSpec((N_SLAB, 1, LANES), lambda i: (0, 0, 0),
                             pipeline_mode=pl.Buffered(1)),
                row(W_CONV), row(W_ATT),
                pl.BlockSpec((None, tm, PLE_DIM), lambda i: (layer, i, 0)),
                const((1, W_CONV)), const((1, W_CONV)),
                const((W_CONV, D_MODEL), 0), const((W_ATT, D_MODEL), 1),
                pl.BlockSpec((None, PLE_DIM, D_MODEL), lambda i: (layer, 0, 0),
                             pipeline_mode=pl.Buffered(1)),
                const((1, D_MODEL)), const((D_MODEL, D_MODEL)), const((1, D_MODEL))]
    args = [h, yc, cb, gz, yb, p, lg, lb, wo, wo, wpe, peg, wpg, ng]
    if last:
        out_specs = [row(D_MODEL)]
        out_shape = [jax.ShapeDtypeStruct((T, D_MODEL), F32)]
    else:
        in_specs += [pl.BlockSpec((None, wrows, w_in_cols), lambda i: (layer + 1, i, 0)),
                     const((1, w_in_cols))]
        args += [w_in32, w_scale]
        out_specs = [row(D_MODEL), row(D_MODEL),
                     pl.BlockSpec((wrows, w_in_cols), lambda i: (i, 0))]
        out_shape = [jax.ShapeDtypeStruct((T, D_MODEL), F32),
                     jax.ShapeDtypeStruct((T, D_MODEL), BF16),
                     jax.ShapeDtypeStruct((D_MODEL, w_in_cols), BF16)]
    return pl.pallas_call(
        functools.partial(_outproj_kernel, last=last),
        grid=(steps,),
        in_specs=in_specs,
        out_specs=out_specs,
        out_shape=out_shape,
        compiler_params=pltpu.CompilerParams(
            dimension_semantics=("arbitrary",),
            vmem_limit_bytes=VMEM_LIMIT),
        name="outproj_final" if last else "outproj",
    )(*args)


def _band_buckets():
    q_off = np.arange(BLK)[:, None]
    k_off = np.arange(3 * BLK)[None, :] - BLK
    rel = k_off - q_off
    half = NUM_BUCKETS // 2
    ret = (rel > 0).astype(np.int32) * half
    n = np.abs(rel)
    max_exact = half // 2
    large = max_exact + (np.log(np.maximum(n, 1) / max_exact)
                         / np.log(MAX_DISTANCE / max_exact)
                         * (half - max_exact)).astype(np.int32)
    large = np.minimum(large, half - 1)
    ret = ret + np.where(n < max_exact, n, large)
    return ret.astype(np.int32), (n <= WINDOW)


def _bias_table(rel_bias):
    buckets, band = _band_buckets()
    onehot = np.zeros((BLK * 3 * BLK, NUM_BUCKETS), np.float32)
    onehot[np.arange(onehot.shape[0]), buckets.reshape(-1)] = 1.0
    bias = jnp.dot(rel_bias.astype(F32).T, jnp.asarray(onehot.T),
                   precision=lax.Precision.HIGHEST)
    bias = bias.reshape(N_Q_HEADS, BLK, 3 * BLK)
    bias = jnp.where(jnp.asarray(band)[None], bias * LOG2E, NEG)
    col = np.arange(3 * BLK)
    first = jnp.where(jnp.asarray(col < BLK)[None, None], NEG, bias)
    last = jnp.where(jnp.asarray(col >= 2 * BLK)[None, None], NEG, bias)
    tab = jnp.stack([first, bias, last])
    tab = tab.reshape(3, N_Q_HEADS // 2, 2, BLK, 3 * BLK)
    tab = jnp.concatenate([tab[:, :, 0], tab[:, :, 1]], axis=-1)
    return tab.reshape(3, N_KV_HEADS, 2 * BLK, 6 * BLK)


def _w_in_scale():
    q0 = 3 * W_CONV
    col = np.arange(3 * W_CONV + 2 * W_ATT + 2 * W_KV)
    scale = np.where((col >= q0) & (col < q0 + W_ATT), HEAD_DIM ** -0.5 * LOG2E, 1.0)
    return jnp.asarray(scale, F32).reshape(1, -1)


def kernel(x, p, norm_g, w_in, conv_w, conv_b, cln_g, cln_b, sink, rel_bias,
           w_out, w_pe, pe_g, w_pg, final_g):
    B, S, _ = x.shape
    T = B * S
    bias4 = _bias_table(rel_bias)
    w_scale = _w_in_scale()
    win = (w_in[0] * w_scale).astype(BF16)
    wpe = w_pe.astype(BF16)
    cw = jnp.transpose(conv_w.reshape(DEPTH, CONV_WIDTH, N_SLAB, LANES), (0, 2, 1, 3))
    p3 = p.reshape(DEPTH, T, PLE_DIM)
    h = x.reshape(T, D_MODEL)
    hn = h
    for i in range(DEPTH):
        a, gz, gb, q, kv, wo, wpg = _inproj(hn, norm_g[i].reshape(1, D_MODEL), win, w_out, w_pg,
                                            i, fuse_norm=(i == 0))
        yc = _conv_branch(a.reshape(N_SLAB, B, S, LANES), cw[i])
        yb = _attn_branch(sink[i] * LOG2E, q.reshape(B, S, W_ATT),
                          kv.reshape(B, S, N_KV_HEADS * HW), gb.reshape(B, S, W_ATT), bias4)
        last = i == DEPTH - 1
        ng = final_g if last else norm_g[i + 1]
        outs = _outproj(h, yc.reshape(N_SLAB, T, LANES), conv_b[i].reshape(N_SLAB, 1, LANES),
                        gz, yb.reshape(T, W_ATT), p3,
                        cln_g[i].reshape(1, W_CONV), cln_b[i].reshape(1, W_CONV),
                        wo, wpe, pe_g[i].reshape(1, D_MODEL), wpg, ng.reshape(1, D_MODEL),
                        w_in, w_scale, i, last=last)
        if last:
            (h,) = outs
        else:
            h, hn, win = outs
    return h.reshape(B, S, D_MODEL)
```

```python
import functools

import numpy as np
import jax
import jax.numpy as jnp
from jax import lax
from jax.experimental import pallas as pl
from jax.experimental.pallas import tpu as pltpu

D_MODEL = 2048
DEPTH = 4
W_CONV = 1024
HEAD_DIM = 64
N_Q_HEADS = 16
N_KV_HEADS = 4
GQA_GROUP = 4
W_ATT = 1024
W_KV = 256
CONV_WIDTH = 31
CONV_PAD = 15
WINDOW = 128
BLK = 128
NUM_BUCKETS = 32
MAX_DISTANCE = 128
PLE_DIM = 256
EPS = 1e-6
NEG = -1e30
LOG2E = float(np.log2(np.e))

LANES = 128
SUBLANES = 8
HALO = 16
N_SLAB = W_CONV // LANES
HW = GQA_GROUP * HEAD_DIM
MXU_N = 256
N_GROUP = 2
GW = W_ATT // N_GROUP
KVW = (2 * W_KV) // N_GROUP
assert KVW % MXU_N == 0 and GW % MXU_N == 0
VMEM_LIMIT = 62 * 1024 * 1024

F32 = jnp.float32
BF16 = jnp.bfloat16


def _sigmoid(x):
    return jax.nn.sigmoid(x)


def _silu(x):
    return x * jax.nn.sigmoid(x)


def _rmsnorm(x, g):
    ms = jnp.mean(x * x, axis=-1, keepdims=True)
    return (x * lax.rsqrt(ms + EPS)) * g


def _inproj_kernel(x_ref, g_ref, wav_ref, wag_ref, waz_ref, wbz_ref, wq_ref, wk_ref, wv_ref,
                   wo32_ref, wpg32_ref, a_ref, gz_ref, gb_ref, q_ref, kv_ref, wo_ref, wpg_ref,
                   *, fuse_norm):
    wo_ref[...] = wo32_ref[...].astype(BF16)
    wpg_ref[...] = wpg32_ref[...].astype(BF16)
    if fuse_norm:
        hn = _rmsnorm(x_ref[...], g_ref[...]).astype(BF16)
    else:
        hn = x_ref[...]

    def proj(w):
        return jnp.dot(hn, w, preferred_element_type=F32)

    a = proj(wav_ref[...]) * _sigmoid(proj(wag_ref[...]))
    for s in range(GW // LANES):
        a_ref[s] = a[:, s * LANES:(s + 1) * LANES]
    gz_ref[...] = _silu(proj(waz_ref[...]))
    gb_ref[...] = _silu(proj(wbz_ref[...]))
    q_ref[...] = proj(wq_ref[...]).astype(BF16)
    kv = proj(jnp.concatenate([wk_ref[...], wv_ref[...]], axis=1))
    k2, v2 = kv[:, :LANES], kv[:, LANES:]
    k2r, v2r = pltpu.roll(k2, HEAD_DIM, 1), pltpu.roll(v2, HEAD_DIM, 1)
    lo = lax.broadcasted_iota(jnp.int32, k2.shape, 1) < HEAD_DIM
    pieces = [jnp.where(lo, k2, v2r), jnp.where(lo, v2, k2r),
              jnp.where(lo, k2r, v2), jnp.where(lo, v2r, k2)]
    for s, piece in enumerate(pieces):
        kv_ref[:, s * LANES:(s + 1) * LANES] = piece.astype(BF16)


def _inproj(x, g, w_in, w_out, w_pg, layer, *, fuse_norm, tm=1024):
    T = x.shape[0]
    steps = (T // tm) * N_GROUP
    wrows = D_MODEL // steps
    assert wrows * steps == D_MODEL and wrows % 16 == 0

    def col(dtype, width=GW):
        return (pl.BlockSpec((tm, width), lambda i, j: (i, j)),
                jax.ShapeDtypeStruct((T, N_GROUP * width), dtype))

    def wcols(offset, width):
        base = offset // width
        assert base * width == offset
        return pl.BlockSpec((D_MODEL, width), lambda i, j: (0, base + j))

    cast_in = pl.BlockSpec((None, wrows, D_MODEL), lambda i, j: (layer, i * N_GROUP + j, 0))
    cast_out = (pl.BlockSpec((wrows, D_MODEL), lambda i, j: (i * N_GROUP + j, 0)),
                jax.ShapeDtypeStruct((D_MODEL, D_MODEL), BF16))

    o = np.cumsum([0, W_CONV, W_CONV, W_CONV, W_ATT, W_KV, W_KV])
    kw = W_KV // N_GROUP
    w_specs = [wcols(o[0], GW), wcols(o[1], GW), wcols(o[2], GW), wcols(o[6], GW),
               wcols(o[3], GW), wcols(o[4], kw), wcols(o[5], kw)]
    specs, shapes = zip(col(F32), col(F32), col(BF16), col(BF16, 2 * KVW), cast_out, cast_out)
    spg = GW // LANES
    return pl.pallas_call(
        functools.partial(_inproj_kernel, fuse_norm=fuse_norm),
        grid=(T // tm, N_GROUP),
        in_specs=[
            pl.BlockSpec((tm, D_MODEL), lambda i, j: (i, 0)),
            pl.BlockSpec((1, D_MODEL), lambda i, j: (0, 0)),
        ] + w_specs + [cast_in, cast_in],
        out_specs=[pl.BlockSpec((spg, tm, LANES), lambda i, j: (j, i, 0))] + list(specs),
        out_shape=[jax.ShapeDtypeStruct((N_SLAB, T, LANES), F32)] + list(shapes),
        compiler_params=pltpu.CompilerParams(
            dimension_semantics=("arbitrary", "arbitrary"),
            vmem_limit_bytes=VMEM_LIMIT),
        name="inproj_norm" if fuse_norm else "inproj",
    )(x, g, *([w_in] * len(w_specs)), w_out, w_pg)


def _conv_kernel(ap_ref, ac_ref, an_ref, cw_ref, o_ref, a_ext, *, ts, rc):
    i = pl.program_id(1)
    n = pl.num_programs(1)
    zero = jnp.zeros((N_SLAB, HALO, LANES), F32)
    a_ext[:, 0:HALO, :] = jnp.where(i > 0, ap_ref[:, 0], zero)
    a_ext[:, HALO:HALO + ts, :] = ac_ref[:, 0]
    a_ext[:, HALO + ts:, :] = jnp.where(i < n - 1, an_ref[:, 0], zero)

    n_sub = rc // SUBLANES
    first = HALO - CONV_PAD
    n_off = CONV_WIDTH + (n_sub - 1) * SUBLANES

    for c in range(N_SLAB):
        taps = [jnp.broadcast_to(cw_ref[c, k:k + 1, :], (SUBLANES, LANES))
                for k in range(CONV_WIDTH)]

        def conv_chunk(r, carry, c=c, taps=taps):
            r0 = pl.multiple_of(r * rc, rc)
            accs = [[None, None] for _ in range(n_sub)]
            for o in range(n_off):
                win = a_ext[c, pl.ds(r0 + first + o, SUBLANES, stride=1), :]
                for j in range(n_sub):
                    k = o - j * SUBLANES
                    if 0 <= k < CONV_WIDTH:
                        prod = win * taps[k]
                        cur = accs[j][k % 2]
                        accs[j][k % 2] = prod if cur is None else cur + prod
            o_ref[c, 0, pl.ds(r0, rc), :] = jnp.concatenate([e + o_ for e, o_ in accs], axis=0)
            return carry

        lax.fori_loop(0, ts // rc, conv_chunk, 0)


def _conv_branch(a4, cw, *, ts=1024, rc=128):
    _, B, S, _ = a4.shape
    nh = ts // HALO
    last_h = S // HALO - 1

    def const(shape):
        return pl.BlockSpec(shape, lambda b, i: (0,) * len(shape))

    return pl.pallas_call(
        functools.partial(_conv_kernel, ts=ts, rc=rc),
        grid=(B, S // ts),
        in_specs=[
            pl.BlockSpec((N_SLAB, 1, HALO, LANES),
                         lambda b, i: (0, b, jnp.maximum(i * nh - 1, 0), 0)),
            pl.BlockSpec((N_SLAB, 1, ts, LANES), lambda b, i: (0, b, i, 0)),
            pl.BlockSpec((N_SLAB, 1, HALO, LANES),
                         lambda b, i: (0, b, jnp.minimum((i + 1) * nh, last_h), 0)),
            const((N_SLAB, CONV_WIDTH, LANES))],
        out_specs=pl.BlockSpec((N_SLAB, 1, ts, LANES), lambda b, i: (0, b, i, 0)),
        out_shape=jax.ShapeDtypeStruct((N_SLAB, B, S, LANES), F32),
        scratch_shapes=[pltpu.VMEM((N_SLAB, ts + 2 * HALO, LANES), F32)],
        compiler_params=pltpu.CompilerParams(
            dimension_semantics=("arbitrary", "arbitrary"),
            vmem_limit_bytes=VMEM_LIMIT),
        name="conv_branch",
    )(a4, a4, a4, cw)


def _attn_kernel(sink_ref, q_ref, kvp_ref, kvc_ref, kvn_ref, gb_ref, bias_ref, o_ref, *, qb):
    step = pl.program_id(1)
    last_blk = pl.num_programs(1) * qb - 1
    nk = 3 * BLK
    lo = lax.broadcasted_iota(jnp.int32, (nk, LANES), 1) < HEAD_DIM
    lo_q = lax.broadcasted_iota(jnp.int32, (2 * BLK, LANES), 1) < HEAD_DIM
    top = lax.broadcasted_iota(jnp.int32, (2 * BLK, 1), 0) < BLK
    zeros = jnp.zeros((nk, LANES), BF16)
    ones_lo = jnp.where(lo, 1.0, 0.0).astype(BF16)
    ones_hi = jnp.where(lo, 0.0, 1.0).astype(BF16)
    sum_cols = jnp.concatenate([ones_lo, ones_hi], axis=0)

    def keys(sub, cols):
        parts = []
        for t in (sub - 1, sub, sub + 1):
            if t < 0:
                parts.append(kvp_ref[0, :, cols])
            elif t >= qb:
                parts.append(kvn_ref[0, :, cols])
            else:
                parts.append(kvc_ref[0, t * BLK:(t + 1) * BLK, cols])
        return jnp.concatenate(parts, axis=0)

    for sub, h in [(sub, h) for sub in range(qb) for h in range(N_KV_HEADS)]:
        blk = step * qb + sub
        variant = jnp.where(blk == 0, 0, jnp.where(blk == last_blk, 2, 1))
        rows = slice(sub * BLK, (sub + 1) * BLK)
        kv = keys(sub, slice(h * HW, h * HW + LANES))
        vk = keys(sub, slice(h * HW + LANES, (h + 1) * HW))
        kblk = jnp.concatenate([jnp.where(lo, kv, zeros), jnp.where(lo, zeros, vk)], axis=0)
        vblk = jnp.concatenate([jnp.where(lo, vk, zeros), jnp.where(lo, zeros, kv)], axis=0)
        vblk = jnp.concatenate([vblk, sum_cols], axis=1)
        hs = slice(h * HW, (h + 1) * HW)
        qh = q_ref[0, rows, hs]
        lhs = jnp.concatenate([qh[:, :LANES], qh[:, LANES:]], axis=0)
        s = lax.dot_general(lhs, kblk, (((1,), (1,)), ((), ())), preferred_element_type=F32)
        s = s + bias_ref[variant, h]
        sa, sb = s[:, :nk], s[:, nk:]
        g0 = h * GQA_GROUP
        sk_a = jnp.where(top, sink_ref[g0], sink_ref[g0 + 2])
        sk_b = jnp.where(top, sink_ref[g0 + 1], sink_ref[g0 + 3])
        ma = jnp.maximum(jnp.max(sa, axis=-1, keepdims=True), sk_a)
        mb = jnp.maximum(jnp.max(sb, axis=-1, keepdims=True), sk_b)
        pe = jnp.concatenate([jnp.exp2(sa - ma), jnp.exp2(sb - mb)], axis=1).astype(BF16)
        o = jnp.dot(pe, vblk, preferred_element_type=F32)
        den = o[:, LANES:] + jnp.where(lo_q, jnp.exp2(sk_a - ma), jnp.exp2(sk_b - mb))
        y = o[:, :LANES] * (1.0 / den)
        yh = jnp.concatenate([y[:BLK], y[BLK:]], axis=1)
        o_ref[0, rows, hs] = (yh * gb_ref[0, rows, hs]).astype(BF16)


def _attn_branch(sink, q3, kv3, gb3, bias4, *, qb=16):
    B, S, _ = q3.shape
    nb = S // BLK
    kvw = kv3.shape[-1]

    def row(width=W_ATT):
        return pl.BlockSpec((1, qb * BLK, width), lambda b, j: (b, j, 0))

    def halo(d):
        return pl.BlockSpec((1, BLK, kvw),
                            lambda b, j: (b, jnp.clip(j * qb + d, 0, nb - 1), 0))

    return pl.pallas_call(
        functools.partial(_attn_kernel, qb=qb),
        grid=(B, nb // qb),
        in_specs=[
            pl.BlockSpec(memory_space=pltpu.SMEM),
            row(), halo(-1), row(kvw), halo(qb), row(),
            pl.BlockSpec(bias4.shape, lambda b, j: (0, 0, 0, 0), pipeline_mode=pl.Buffered(1)),
        ],
        out_specs=row(),
        out_shape=jax.ShapeDtypeStruct((B, S, W_ATT), BF16),
        compiler_params=pltpu.CompilerParams(
            dimension_semantics=("arbitrary", "arbitrary"),
            vmem_limit_bytes=VMEM_LIMIT),
        name="attn_branch",
    )(sink, q3, kv3, kv3, kv3, gb3, bias4)


def _outproj_kernel(h_ref, yc_ref, cb_ref, gz_ref, yb_ref, p_ref, lg_ref, lb_ref, woa_ref,
                    wob_ref, wpe_ref, peg_ref, wpg_ref, ng_ref, *rest, last):
    if last:
        (o_ref,) = rest
    else:
        win32_ref, scale_ref, h_out, hn_out, win_ref = rest
        win_ref[...] = (win32_ref[...] * scale_ref[...]).astype(BF16)
    ys = [yc_ref[c] + cb_ref[c] for c in range(N_SLAB)]
    tot = ys[0]
    for c in range(1, N_SLAB):
        tot = tot + ys[c]
    mu = jnp.sum(tot, axis=-1, keepdims=True) * (1.0 / W_CONV)
    xcs = [y - mu for y in ys]
    sq = xcs[0] * xcs[0]
    for c in range(1, N_SLAB):
        sq = sq + xcs[c] * xcs[c]
    var = jnp.sum(sq, axis=-1, keepdims=True) * (1.0 / W_CONV)
    rs = lax.rsqrt(var + EPS)
    ya = []
    for c in range(N_SLAB):
        cs = slice(c * LANES, (c + 1) * LANES)
        y = (xcs[c] * rs) * lg_ref[:, cs] + lb_ref[:, cs]
        ya.append((_silu(y) * gz_ref[:, cs]).astype(BF16))
    ya = jnp.concatenate(ya, axis=1)

    h1 = h_ref[...] + jnp.dot(yb_ref[...], wob_ref[...], preferred_element_type=F32)
    h1 = h1 + jnp.dot(ya, woa_ref[...], preferred_element_type=F32)
    e = jnp.dot(p_ref[...].astype(BF16), wpe_ref[...], preferred_element_type=F32)
    e = _rmsnorm(e, peg_ref[...])
    gate = jnp.dot(h1.astype(BF16), wpg_ref[...], preferred_element_type=F32)
    h2 = h1 + e * _sigmoid(gate)
    if last:
        o_ref[...] = _rmsnorm(h2, ng_ref[...])
    else:
        h_out[...] = h2
        hn_out[...] = _rmsnorm(h2, ng_ref[...]).astype(BF16)


def _outproj(h, yc, cb, gz, yb, p, lg, lb, wo, wpe, peg, wpg, ng, w_in32, w_scale, layer, *,
             last, tm=512):
    T = h.shape[0]
    steps = T // tm
    wrows = D_MODEL // steps
    assert wrows * steps == D_MODEL and wrows % 16 == 0
    w_in_cols = w_in32.shape[-1]

    def row(width):
        return pl.BlockSpec((tm, width), lambda i: (i, 0))

    def const(shape, blk=0):
        return pl.BlockSpec(shape, lambda i: (blk, 0), pipeline_mode=pl.Buffered(1))

    in_specs = [row(D_MODEL),
                pl.BlockSpec((N_SLAB, tm, LANES), lambda i: (0, i, 0)),
                pl.BlockSpec((N_SLAB, 1, LANES), lambda i: (0, 0, 0),
                             pipeline_mode=pl.Buffered(1)),
                row(W_CONV), row(W_ATT),
                pl.BlockSpec((None, tm, PLE_DIM), lambda i: (layer, i, 0)),
                const((1, W_CONV)), const((1, W_CONV)),
                const((W_CONV, D_MODEL), 0), const((W_ATT, D_MODEL), 1),
                pl.BlockSpec((None, PLE_DIM, D_MODEL), lambda i: (layer, 0, 0),
                             pipeline_mode=pl.Buffered(1)),
                const((1, D_MODEL)), const((D_MODEL, D_MODEL)), const((1, D_MODEL))]
    args = [h, yc, cb, gz, yb, p, lg, lb, wo, wo, wpe, peg, wpg, ng]
    if last:
        out_specs = [row(D_MODEL)]
        out_shape = [jax.ShapeDtypeStruct((T, D_MODEL), F32)]
    else:
        in_specs += [pl.BlockSpec((None, wrows, w_in_cols), lambda i: (layer + 1, i, 0)),
                     const((1, w_in_cols))]
        args += [w_in32, w_scale]
        out_specs = [row(D_MODEL), row(D_MODEL),
                     pl.BlockSpec((wrows, w_in_cols), lambda i: (i, 0))]
        out_shape = [jax.ShapeDtypeStruct((T, D_MODEL), F32),
                     jax.ShapeDtypeStruct((T, D_MODEL), BF16),
                     jax.ShapeDtypeStruct((D_MODEL, w_in_cols), BF16)]
    return pl.pallas_call(
        functools.partial(_outproj_kernel, last=last),
        grid=(steps,),
        in_specs=in_specs,
        out_specs=out_specs,
        out_shape=out_shape,
        compiler_params=pltpu.CompilerParams(
            dimension_semantics=("arbitrary",),
            vmem_limit_bytes=VMEM_LIMIT),
        name="outproj_final" if last else "outproj",
    )(*args)


def _band_buckets():
    q_off = np.arange(BLK)[:, None]
    k_off = np.arange(3 * BLK)[None, :] - BLK
    rel = k_off - q_off
    half = NUM_BUCKETS // 2
    ret = (rel > 0).astype(np.int32) * half
    n = np.abs(rel)
    max_exact = half // 2
    large = max_exact + (np.log(np.maximum(n, 1) / max_exact)
                         / np.log(MAX_DISTANCE / max_exact)
                         * (half - max_exact)).astype(np.int32)
    large = np.minimum(large, half - 1)
    ret = ret + np.where(n < max_exact, n, large)
    return ret.astype(np.int32), (n <= WINDOW)


def _bias_table(rel_bias):
    buckets, band = _band_buckets()
    onehot = np.zeros((BLK * 3 * BLK, NUM_BUCKETS), np.float32)
    onehot[np.arange(onehot.shape[0]), buckets.reshape(-1)] = 1.0
    bias = jnp.dot(rel_bias.astype(F32).T, jnp.asarray(onehot.T),
                   precision=lax.Precision.HIGHEST)
    bias = bias.reshape(N_Q_HEADS, BLK, 3 * BLK)
    bias = jnp.where(jnp.asarray(band)[None], bias * LOG2E, NEG)
    col = np.arange(3 * BLK)
    first = jnp.where(jnp.asarray(col < BLK)[None, None], NEG, bias)
    last = jnp.where(jnp.asarray(col >= 2 * BLK)[None, None], NEG, bias)
    tab = jnp.stack([first, bias, last])
    tab = tab.reshape(3, N_Q_HEADS // 2, 2, BLK, 3 * BLK)
    tab = jnp.concatenate([tab[:, :, 0], tab[:, :, 1]], axis=-1)
    return tab.reshape(3, N_KV_HEADS, 2 * BLK, 6 * BLK)


def _w_in_scale():
    q0 = 3 * W_CONV
    col = np.arange(3 * W_CONV + 2 * W_ATT + 2 * W_KV)
    scale = np.where((col >= q0) & (col < q0 + W_ATT), HEAD_DIM ** -0.5 * LOG2E, 1.0)
    return jnp.asarray(scale, F32).reshape(1, -1)


def kernel(x, p, norm_g, w_in, conv_w, conv_b, cln_g, cln_b, sink, rel_bias,
           w_out, w_pe, pe_g, w_pg, final_g):
    B, S, _ = x.shape
    T = B * S
    bias4 = _bias_table(rel_bias)
    w_scale = _w_in_scale()
    win = (w_in[0] * w_scale).astype(BF16)
    wpe = w_pe.astype(BF16)
    cw = jnp.transpose(conv_w.reshape(DEPTH, CONV_WIDTH, N_SLAB, LANES), (0, 2, 1, 3))
    p3 = p.reshape(DEPTH, T, PLE_DIM)
    h = x.reshape(T, D_MODEL)
    hn = h
    for i in range(DEPTH):
        a, gz, gb, q, kv, wo, wpg = _inproj(hn, norm_g[i].reshape(1, D_MODEL), win, w_out, w_pg,
                                            i, fuse_norm=(i == 0))
        yc = _conv_branch(a.reshape(N_SLAB, B, S, LANES), cw[i])
        yb = _attn_branch(sink[i] * LOG2E, q.reshape(B, S, W_ATT),
                          kv.reshape(B, S, N_KV_HEADS * HW), gb.reshape(B, S, W_ATT), bias4)
        last = i == DEPTH - 1
        ng = final_g if last else norm_g[i + 1]
        outs = _outproj(h, yc.reshape(N_SLAB, T, LANES), conv_b[i].reshape(N_SLAB, 1, LANES),
                        gz, yb.reshape(T, W_ATT), p3,
                        cln_g[i].reshape(1, W_CONV), cln_b[i].reshape(1, W_CONV),
                        wo, wpe, pe_g[i].reshape(1, D_MODEL), wpg, ng.reshape(1, D_MODEL),
                        w_in, w_scale, i, last=last)
        if last:
            (h,) = outs
        else:
            h, hn, win = outs
    return h.reshape(B, S, D_MODEL)
```

```python
import functools

import numpy as np
import jax
import jax.numpy as jnp
from jax import lax
from jax.experimental import pallas as pl
from jax.experimental.pallas import tpu as pltpu

D_MODEL = 2048
DEPTH = 4
W_CONV = 1024
HEAD_DIM = 64
N_Q_HEADS = 16
N_KV_HEADS = 4
GQA_GROUP = 4
W_ATT = 1024
W_KV = 256
CONV_WIDTH = 31
CONV_PAD = 15
WINDOW = 128
BLK = 128
NUM_BUCKETS = 32
MAX_DISTANCE = 128
PLE_DIM = 256
EPS = 1e-6
NEG = -1e30
LOG2E = float(np.log2(np.e))

LANES = 128
SUBLANES = 8
HALO = 16
N_SLAB = W_CONV // LANES
HW = GQA_GROUP * HEAD_DIM
MXU_N = 256
N_GROUP = 2
GW = W_ATT // N_GROUP
KVW = (2 * W_KV) // N_GROUP
assert KVW % MXU_N == 0 and GW % MXU_N == 0
VMEM_LIMIT = 62 * 1024 * 1024

F32 = jnp.float32
BF16 = jnp.bfloat16


def _sigmoid(x):
    return jax.nn.sigmoid(x)


def _silu(x):
    return x * jax.nn.sigmoid(x)


def _rmsnorm(x, g):
    ms = jnp.mean(x * x, axis=-1, keepdims=True)
    return (x * lax.rsqrt(ms + EPS)) * g


def _inproj_kernel(x_ref, g_ref, wav_ref, wag_ref, waz_ref, wbz_ref, wq_ref, wk_ref, wv_ref,
                   wo32_ref, wpg32_ref, a_ref, gz_ref, gb_ref, q_ref, kv_ref, wo_ref, wpg_ref,
                   *, fuse_norm):
    wo_ref[...] = wo32_ref[...].astype(BF16)
    wpg_ref[...] = wpg32_ref[...].astype(BF16)
    if fuse_norm:
        hn = _rmsnorm(x_ref[...], g_ref[...]).astype(BF16)
    else:
        hn = x_ref[...]

    def proj(w):
        return jnp.dot(hn, w, preferred_element_type=F32)

    a = proj(wav_ref[...]) * _sigmoid(proj(wag_ref[...]))
    for s in range(GW // LANES):
        a_ref[s] = a[:, s * LANES:(s + 1) * LANES]
    gz_ref[...] = _silu(proj(waz_ref[...]))
    gb_ref[...] = _silu(proj(wbz_ref[...]))
    q_ref[...] = proj(wq_ref[...]).astype(BF16)
    kv = proj(jnp.concatenate([wk_ref[...], wv_ref[...]], axis=1))
    k2, v2 = kv[:, :LANES], kv[:, LANES:]
    k2r, v2r = pltpu.roll(k2, HEAD_DIM, 1), pltpu.roll(v2, HEAD_DIM, 1)
    lo = lax.broadcasted_iota(jnp.int32, k2.shape, 1) < HEAD_DIM
    pieces = [jnp.where(lo, k2, v2r), jnp.where(lo, v2, k2r),
              jnp.where(lo, k2r, v2), jnp.where(lo, v2r, k2)]
    for s, piece in enumerate(pieces):
        kv_ref[:, s * LANES:(s + 1) * LANES] = piece.astype(BF16)


def _inproj(x, g, w_in, w_out, w_pg, layer, *, fuse_norm, tm=1024):
    T = x.shape[0]
    steps = (T // tm) * N_GROUP
    wrows = D_MODEL // steps
    assert wrows * steps == D_MODEL and wrows % 16 == 0

    def col(dtype, width=GW):
        return (pl.BlockSpec((tm, width), lambda i, j: (i, j)),
                jax.ShapeDtypeStruct((T, N_GROUP * width), dtype))

    def wcols(offset, width):
        base = offset // width
        assert base * width == offset
        return pl.BlockSpec((D_MODEL, width), lambda i, j: (0, base + j))

    cast_in = pl.BlockSpec((None, wrows, D_MODEL), lambda i, j: (layer, i * N_GROUP + j, 0))
    cast_out = (pl.BlockSpec((wrows, D_MODEL), lambda i, j: (i * N_GROUP + j, 0)),
                jax.ShapeDtypeStruct((D_MODEL, D_MODEL), BF16))

    o = np.cumsum([0, W_CONV, W_CONV, W_CONV, W_ATT, W_KV, W_KV])
    kw = W_KV // N_GROUP
    w_specs = [wcols(o[0], GW), wcols(o[1], GW), wcols(o[2], GW), wcols(o[6], GW),
               wcols(o[3], GW), wcols(o[4], kw), wcols(o[5], kw)]
    specs, shapes = zip(col(F32), col(F32), col(BF16), col(BF16, 2 * KVW), cast_out, cast_out)
    spg = GW // LANES
    return pl.pallas_call(
        functools.partial(_inproj_kernel, fuse_norm=fuse_norm),
        grid=(T // tm, N_GROUP),
        in_specs=[
            pl.BlockSpec((tm, D_MODEL), lambda i, j: (i, 0)),
            pl.BlockSpec((1, D_MODEL), lambda i, j: (0, 0)),
        ] + w_specs + [cast_in, cast_in],
        out_specs=[pl.BlockSpec((spg, tm, LANES), lambda i, j: (j, i, 0))] + list(specs),
        out_shape=[jax.ShapeDtypeStruct((N_SLAB, T, LANES), F32)] + list(shapes),
        compiler_params=pltpu.CompilerParams(
            dimension_semantics=("arbitrary", "arbitrary"),
            vmem_limit_bytes=VMEM_LIMIT),
        name="inproj_norm" if fuse_norm else "inproj",
    )(x, g, *([w_in] * len(w_specs)), w_out, w_pg)


def _conv_kernel(a_ref, cw_ref, o_ref, lo_buf, hi_buf, *, rc):
    seq = a_ref.shape[2]
    n_sub = rc // SUBLANES
    first = HALO - CONV_PAD
    n_off = CONV_WIDTH + (n_sub - 1) * SUBLANES
    edge = rc + HALO
    zero = jnp.zeros((N_SLAB, HALO, LANES), F32)
    lo_buf[:, 0:HALO, :] = zero
    lo_buf[:, HALO:, :] = a_ref[:, 0, 0:edge, :]
    hi_buf[:, 0:edge, :] = a_ref[:, 0, seq - edge:seq, :]
    hi_buf[:, edge:, :] = zero

    for c in range(N_SLAB):
        taps = [jnp.broadcast_to(cw_ref[c, k:k + 1, :], (SUBLANES, LANES))
                for k in range(CONV_WIDTH)]

        def conv_chunk(window, c=c, taps=taps):
            accs = [[None, None] for _ in range(n_sub)]
            for o in range(n_off):
                win = window(o)
                for j in range(n_sub):
                    k = o - j * SUBLANES
                    if 0 <= k < CONV_WIDTH:
                        prod = win * taps[k]
                        cur = accs[j][k % 2]
                        accs[j][k % 2] = prod if cur is None else cur + prod
            return jnp.concatenate([e + o_ for e, o_ in accs], axis=0)

        o_ref[c, 0, 0:rc, :] = conv_chunk(
            lambda o, c=c: lo_buf[c, pl.ds(first + o, SUBLANES, stride=1), :])

        def interior(r, carry, c=c, conv_chunk=conv_chunk):
            r0 = pl.multiple_of(r * rc, rc)
            o_ref[c, 0, pl.ds(r0, rc), :] = conv_chunk(
                lambda o: a_ref[c, 0, pl.ds(r0 - CONV_PAD + o, SUBLANES, stride=1), :])
            return carry

        lax.fori_loop(1, seq // rc - 1, interior, 0)
        o_ref[c, 0, seq - rc:seq, :] = conv_chunk(
            lambda o, c=c: hi_buf[c, pl.ds(first + o, SUBLANES, stride=1), :])


def _conv_branch(a4, cw, *, rc=128):
    _, B, S, _ = a4.shape
    seq_block = pl.BlockSpec((N_SLAB, 1, S, LANES), lambda b: (0, b, 0, 0))
    return pl.pallas_call(
        functools.partial(_conv_kernel, rc=rc),
        grid=(B,),
        in_specs=[seq_block, pl.BlockSpec((N_SLAB, CONV_WIDTH, LANES), lambda b: (0, 0, 0))],
        out_specs=seq_block,
        out_shape=jax.ShapeDtypeStruct((N_SLAB, B, S, LANES), F32),
        scratch_shapes=[pltpu.VMEM((N_SLAB, rc + 2 * HALO, LANES), F32),
                        pltpu.VMEM((N_SLAB, rc + 2 * HALO, LANES), F32)],
        compiler_params=pltpu.CompilerParams(
            dimension_semantics=("arbitrary",),
            vmem_limit_bytes=VMEM_LIMIT),
        name="conv_branch",
    )(a4, cw)


def _attn_kernel(sink_ref, q_ref, kvp_ref, kvc_ref, kvn_ref, gb_ref, bias_ref, o_ref, *, qb):
    step = pl.program_id(1)
    last_blk = pl.num_programs(1) * qb - 1
    nk = 3 * BLK
    lo = lax.broadcasted_iota(jnp.int32, (nk, LANES), 1) < HEAD_DIM
    lo_q = lax.broadcasted_iota(jnp.int32, (2 * BLK, LANES), 1) < HEAD_DIM
    top = lax.broadcasted_iota(jnp.int32, (2 * BLK, 1), 0) < BLK
    zeros = jnp.zeros((nk, LANES), BF16)
    ones_lo = jnp.where(lo, 1.0, 0.0).astype(BF16)
    ones_hi = jnp.where(lo, 0.0, 1.0).astype(BF16)
    sum_cols = jnp.concatenate([ones_lo, ones_hi], axis=0)

    def keys(sub, cols):
        parts = []
        for t in (sub - 1, sub, sub + 1):
            if t < 0:
                parts.append(kvp_ref[0, :, cols])
            elif t >= qb:
                parts.append(kvn_ref[0, :, cols])
            else:
                parts.append(kvc_ref[0, t * BLK:(t + 1) * BLK, cols])
        return jnp.concatenate(parts, axis=0)

    for sub, h in [(sub, h) for sub in range(qb) for h in range(N_KV_HEADS)]:
        blk = step * qb + sub
        variant = jnp.where(blk == 0, 0, jnp.where(blk == last_blk, 2, 1))
        rows = slice(sub * BLK, (sub + 1) * BLK)
        kv = keys(sub, slice(h * HW, h * HW + LANES))
        vk = keys(sub, slice(h * HW + LANES, (h + 1) * HW))
        kblk = jnp.concatenate([jnp.where(lo, kv, zeros), jnp.where(lo, zeros, vk)], axis=0)
        vblk = jnp.concatenate([jnp.where(lo, vk, zeros), jnp.where(lo, zeros, kv)], axis=0)
        vblk = jnp.concatenate([vblk, sum_cols], axis=1)
        hs = slice(h * HW, (h + 1) * HW)
        qh = q_ref[0, rows, hs]
        lhs = jnp.concatenate([qh[:, :LANES], qh[:, LANES:]], axis=0)
        s = lax.dot_general(lhs, kblk, (((1,), (1,)), ((), ())), preferred_element_type=F32)
        s = s + bias_ref[variant, h]
        sa, sb = s[:, :nk], s[:, nk:]
        g0 = h * GQA_GROUP
        sk_a = jnp.where(top, sink_ref[g0], sink_ref[g0 + 2])
        sk_b = jnp.where(top, sink_ref[g0 + 1], sink_ref[g0 + 3])
        ma = jnp.maximum(jnp.max(sa, axis=-1, keepdims=True), sk_a)
        mb = jnp.maximum(jnp.max(sb, axis=-1, keepdims=True), sk_b)
        pe = jnp.concatenate([jnp.exp2(sa - ma), jnp.exp2(sb - mb)], axis=1).astype(BF16)
        o = jnp.dot(pe, vblk, preferred_element_type=F32)
        den = o[:, LANES:] + jnp.where(lo_q, jnp.exp2(sk_a - ma), jnp.exp2(sk_b - mb))
        y = o[:, :LANES] * (1.0 / den)
        yh = jnp.concatenate([y[:BLK], y[BLK:]], axis=1)
        o_ref[0, rows, hs] = (yh * gb_ref[0, rows, hs]).astype(BF16)


def _attn_branch(sink, q3, kv3, gb3, bias4, *, qb=16):
    B, S, _ = q3.shape
    nb = S // BLK
    kvw = kv3.shape[-1]

    def row(width=W_ATT):
        return pl.BlockSpec((1, qb * BLK, width), lambda b, j: (b, j, 0))

    def halo(d):
        return pl.BlockSpec((1, BLK, kvw),
                            lambda b, j: (b, jnp.clip(j * qb + d, 0, nb - 1), 0))

    return pl.pallas_call(
        functools.partial(_attn_kernel, qb=qb),
        grid=(B, nb // qb),
        in_specs=[
            pl.BlockSpec(memory_space=pltpu.SMEM),
            row(), halo(-1), row(kvw), halo(qb), row(),
            pl.BlockSpec(bias4.shape, lambda b, j: (0, 0, 0, 0), pipeline_mode=pl.Buffered(1)),
        ],
        out_specs=row(),
        out_shape=jax.ShapeDtypeStruct((B, S, W_ATT), BF16),
        compiler_params=pltpu.CompilerParams(
            dimension_semantics=("arbitrary", "arbitrary"),
            vmem_limit_bytes=VMEM_LIMIT),
        name="attn_branch",
    )(sink, q3, kv3, kv3, kv3, gb3, bias4)


def _outproj_kernel(h_ref, yc_ref, cb_ref, gz_ref, yb_ref, p_ref, lg_ref, lb_ref, woa_ref,
                    wob_ref, wpe_ref, peg_ref, wpg_ref, ng_ref, *rest, last):
    if last:
        (o_ref,) = rest
    else:
        win32_ref, scale_ref, h_out, hn_out, win_ref = rest
        win_ref[...] = (win32_ref[...] * scale_ref[...]).astype(BF16)
    ys = [yc_ref[c] + cb_ref[c] for c in range(N_SLAB)]
    tot = ys[0]
    for c in range(1, N_SLAB):
        tot = tot + ys[c]
    mu = jnp.sum(tot, axis=-1, keepdims=True) * (1.0 / W_CONV)
    xcs = [y - mu for y in ys]
    sq = xcs[0] * xcs[0]
    for c in range(1, N_SLAB):
        sq = sq + xcs[c] * xcs[c]
    var = jnp.sum(sq, axis=-1, keepdims=True) * (1.0 / W_CONV)
    rs = lax.rsqrt(var + EPS)
    ya = []
    for c in range(N_SLAB):
        cs = slice(c * LANES, (c + 1) * LANES)
        y = (xcs[c] * rs) * lg_ref[:, cs] + lb_ref[:, cs]
        ya.append((_silu(y) * gz_ref[:, cs]).astype(BF16))
    ya = jnp.concatenate(ya, axis=1)

    h1 = h_ref[...] + jnp.dot(yb_ref[...], wob_ref[...], preferred_element_type=F32)
    h1 = h1 + jnp.dot(ya, woa_ref[...], preferred_element_type=F32)
    e = jnp.dot(p_ref[...].astype(BF16), wpe_ref[...], preferred_element_type=F32)
    e = _rmsnorm(e, peg_ref[...])
    gate = jnp.dot(h1.astype(BF16), wpg_ref[...], preferred_element_type=F32)
    h2 = h1 + e * _sigmoid(gate)
    if last:
        o_ref[...] = _rmsnorm(h2, ng_ref[...])
    else:
        h_out[...] = h2
        hn_out[...] = _rmsnorm(h2, ng_ref[...]).astype(BF16)


def _outproj(h, yc, cb, gz, yb, p, lg, lb, wo, wpe, peg, wpg, ng, w_in32, w_scale, layer, *,
             last, tm=512):
    T = h.shape[0]
    steps = T // tm
    wrows = D_MODEL // steps
    assert wrows * steps == D_MODEL and wrows % 16 == 0
    w_in_cols = w_in32.shape[-1]

    def row(width):
        return pl.BlockSpec((tm, width), lambda i: (i, 0))

    def const(shape, blk=0):
        return pl.BlockSpec(shape, lambda i: (blk, 0), pipeline_mode=pl.Buffered(1))

    in_specs = [row(D_MODEL),
                pl.BlockSpec((N_SLAB, tm, LANES), lambda i: (0, i, 0)),
                pl.BlockSpec((N_SLAB, 1, LANES), lambda i: (0, 0, 0),
                             pipeline_mode=pl.Buffered(1)),
                row(W_CONV), row(W_ATT),
                pl.BlockSpec((None, tm, PLE_DIM), lambda i: (layer, i, 0)),
                const((1, W_CONV)), const((1, W_CONV)),
                const((W_CONV, D_MODEL), 0), const((W_ATT, D_MODEL), 1),
                pl.BlockSpec((None, PLE_DIM, D_MODEL), lambda i: (layer, 0, 0),
                             pipeline_mode=pl.Buffered(1)),
                const((1, D_MODEL)), const((D_MODEL, D_MODEL)), const((1, D_MODEL))]
    args = [h, yc, cb, gz, yb, p, lg, lb, wo, wo, wpe, peg, wpg, ng]
    if last:
        out_specs = [row(D_MODEL)]
        out_shape = [jax.ShapeDtypeStruct((T, D_MODEL), F32)]
    else:
        in_specs += [pl.BlockSpec((None, wrows, w_in_cols), lambda i: (layer + 1, i, 0)),
                     const((1, w_in_cols))]
        args += [w_in32, w_scale]
        out_specs = [row(D_MODEL), row(D_MODEL),
                     pl.BlockSpec((wrows, w_in_cols), lambda i: (i, 0))]
        out_shape = [jax.ShapeDtypeStruct((T, D_MODEL), F32),
                     jax.ShapeDtypeStruct((T, D_MODEL), BF16),
                     jax.ShapeDtypeStruct((D_MODEL, w_in_cols), BF16)]
    return pl.pallas_call(
        functools.partial(_outproj_kernel, last=last),
        grid=(steps,),
        in_specs=in_specs,
        out_specs=out_specs,
        out_shape=out_shape,
        compiler_params=pltpu.CompilerParams(
            dimension_semantics=("arbitrary",),
            vmem_limit_bytes=VMEM_LIMIT),
        name="outproj_final" if last else "outproj",
    )(*args)


def _band_buckets():
    q_off = np.arange(BLK)[:, None]
    k_off = np.arange(3 * BLK)[None, :] - BLK
    rel = k_off - q_off
    half = NUM_BUCKETS // 2
    ret = (rel > 0).astype(np.int32) * half
    n = np.abs(rel)
    max_exact = half // 2
    large = max_exact + (np.log(np.maximum(n, 1) / max_exact)
                         / np.log(MAX_DISTANCE / max_exact)
                         * (half - max_exact)).astype(np.int32)
    large = np.minimum(large, half - 1)
    ret = ret + np.where(n < max_exact, n, large)
    return ret.astype(np.int32), (n <= WINDOW)


def _bias_table(rel_bias):
    buckets, band = _band_buckets()
    onehot = np.zeros((BLK * 3 * BLK, NUM_BUCKETS), np.float32)
    onehot[np.arange(onehot.shape[0]), buckets.reshape(-1)] = 1.0
    bias = jnp.dot(rel_bias.astype(F32).T, jnp.asarray(onehot.T),
                   precision=lax.Precision.HIGHEST)
    bias = bias.reshape(N_Q_HEADS, BLK, 3 * BLK)
    bias = jnp.where(jnp.asarray(band)[None], bias * LOG2E, NEG)
    col = np.arange(3 * BLK)
    first = jnp.where(jnp.asarray(col < BLK)[None, None], NEG, bias)
    last = jnp.where(jnp.asarray(col >= 2 * BLK)[None, None], NEG, bias)
    tab = jnp.stack([first, bias, last])
    tab = tab.reshape(3, N_Q_HEADS // 2, 2, BLK, 3 * BLK)
    tab = jnp.concatenate([tab[:, :, 0], tab[:, :, 1]], axis=-1)
    return tab.reshape(3, N_KV_HEADS, 2 * BLK, 6 * BLK)


def _w_in_scale():
    q0 = 3 * W_CONV
    col = np.arange(3 * W_CONV + 2 * W_ATT + 2 * W_KV)
    scale = np.where((col >= q0) & (col < q0 + W_ATT), HEAD_DIM ** -0.5 * LOG2E, 1.0)
    return jnp.asarray(scale, F32).reshape(1, -1)


def kernel(x, p, norm_g, w_in, conv_w, conv_b, cln_g, cln_b, sink, rel_bias,
           w_out, w_pe, pe_g, w_pg, final_g):
    B, S, _ = x.shape
    T = B * S
    bias4 = _bias_table(rel_bias)
    w_scale = _w_in_scale()
    win = (w_in[0] * w_scale).astype(BF16)
    wpe = w_pe.astype(BF16)
    cw = jnp.transpose(conv_w.reshape(DEPTH, CONV_WIDTH, N_SLAB, LANES), (0, 2, 1, 3))
    p3 = p.reshape(DEPTH, T, PLE_DIM)
    h = x.reshape(T, D_MODEL)
    hn = h
    for i in range(DEPTH):
        a, gz, gb, q, kv, wo, wpg = _inproj(hn, norm_g[i].reshape(1, D_MODEL), win, w_out, w_pg,
                                            i, fuse_norm=(i == 0))
        yc = _conv_branch(a.reshape(N_SLAB, B, S, LANES), cw[i])
        yb = _attn_branch(sink[i] * LOG2E, q.reshape(B, S, W_ATT),
                          kv.reshape(B, S, N_KV_HEADS * HW), gb.reshape(B, S, W_ATT), bias4)
        last = i == DEPTH - 1
        ng = final_g if last else norm_g[i + 1]
        outs = _outproj(h, yc.reshape(N_SLAB, T, LANES), conv_b[i].reshape(N_SLAB, 1, LANES),
                        gz, yb.reshape(T, W_ATT), p3,
                        cln_g[i].reshape(1, W_CONV), cln_b[i].reshape(1, W_CONV),
                        wo, wpe, pe_g[i].reshape(1, D_MODEL), wpg, ng.reshape(1, D_MODEL),
                        w_in, w_scale, i, last=last)
        if last:
            (h,) = outs
        else:
            h, hn, win = outs
    return h.reshape(B, S, D_MODEL)
```

```python
import functools

import numpy as np
import jax
import jax.numpy as jnp
from jax import lax
from jax.experimental import pallas as pl
from jax.experimental.pallas import tpu as pltpu

D_MODEL = 2048
DEPTH = 4
W_CONV = 1024
HEAD_DIM = 64
N_Q_HEADS = 16
N_KV_HEADS = 4
GQA_GROUP = 4
W_ATT = 1024
W_KV = 256
CONV_WIDTH = 31
CONV_PAD = 15
WINDOW = 128
BLK = 128
NUM_BUCKETS = 32
MAX_DISTANCE = 128
PLE_DIM = 256
EPS = 1e-6
NEG = -1e30
LOG2E = float(np.log2(np.e))

LANES = 128
SUBLANES = 8
HALO = 16
N_SLAB = W_CONV // LANES
HW = GQA_GROUP * HEAD_DIM
MXU_N = 256
N_GROUP = 2
GW = W_ATT // N_GROUP
KVW = (2 * W_KV) // N_GROUP
assert KVW % MXU_N == 0 and GW % MXU_N == 0
VMEM_LIMIT = 62 * 1024 * 1024

F32 = jnp.float32
BF16 = jnp.bfloat16


def _sigmoid(x):
    return jax.nn.sigmoid(x)


def _silu(x):
    return x * jax.nn.sigmoid(x)


def _rmsnorm(x, g):
    ms = jnp.mean(x * x, axis=-1, keepdims=True)
    return (x * lax.rsqrt(ms + EPS)) * g


def _inproj_kernel(x_ref, g_ref, wav_ref, wag_ref, waz_ref, wbz_ref, wq_ref, wk_ref, wv_ref,
                   wo32_ref, wpg32_ref, a_ref, gz_ref, gb_ref, q_ref, kv_ref, wo_ref, wpg_ref,
                   *, fuse_norm):
    wo_ref[...] = wo32_ref[...].astype(BF16)
    wpg_ref[...] = wpg32_ref[...].astype(BF16)
    if fuse_norm:
        hn = _rmsnorm(x_ref[...], g_ref[...]).astype(BF16)
    else:
        hn = x_ref[...]

    def proj(w):
        return jnp.dot(hn, w, preferred_element_type=F32)

    a = proj(wav_ref[...]) * _sigmoid(proj(wag_ref[...]))
    for s in range(GW // LANES):
        a_ref[s] = a[:, s * LANES:(s + 1) * LANES]
    gz_ref[...] = _silu(proj(waz_ref[...]))
    gb_ref[...] = _silu(proj(wbz_ref[...]))
    q_ref[...] = proj(wq_ref[...]).astype(BF16)
    kv = proj(jnp.concatenate([wk_ref[...], wv_ref[...]], axis=1))
    k2, v2 = kv[:, :LANES], kv[:, LANES:]
    k2r, v2r = pltpu.roll(k2, HEAD_DIM, 1), pltpu.roll(v2, HEAD_DIM, 1)
    lo = lax.broadcasted_iota(jnp.int32, k2.shape, 1) < HEAD_DIM
    pieces = [jnp.where(lo, k2, v2r), jnp.where(lo, v2, k2r),
              jnp.where(lo, k2r, v2), jnp.where(lo, v2r, k2)]
    for s, piece in enumerate(pieces):
        kv_ref[:, s * LANES:(s + 1) * LANES] = piece.astype(BF16)


def _inproj(x, g, w_in, w_out, w_pg, layer, *, fuse_norm, tm=1024):
    T = x.shape[0]
    steps = (T // tm) * N_GROUP
    wrows = D_MODEL // steps
    assert wrows * steps == D_MODEL and wrows % 16 == 0

    def col(dtype, width=GW):
        return (pl.BlockSpec((tm, width), lambda i, j: (i, j)),
                jax.ShapeDtypeStruct((T, N_GROUP * width), dtype))

    def wcols(offset, width):
        base = offset // width
        assert base * width == offset
        return pl.BlockSpec((D_MODEL, width), lambda i, j: (0, base + j))

    cast_in = pl.BlockSpec((None, wrows, D_MODEL), lambda i, j: (layer, i * N_GROUP + j, 0))
    cast_out = (pl.BlockSpec((wrows, D_MODEL), lambda i, j: (i * N_GROUP + j, 0)),
                jax.ShapeDtypeStruct((D_MODEL, D_MODEL), BF16))

    o = np.cumsum([0, W_CONV, W_CONV, W_CONV, W_ATT, W_KV, W_KV])
    kw = W_KV // N_GROUP
    w_specs = [wcols(o[0], GW), wcols(o[1], GW), wcols(o[2], GW), wcols(o[6], GW),
               wcols(o[3], GW), wcols(o[4], kw), wcols(o[5], kw)]
    specs, shapes = zip(col(F32), col(F32), col(BF16), col(BF16, 2 * KVW), cast_out, cast_out)
    spg = GW // LANES
    return pl.pallas_call(
        functools.partial(_inproj_kernel, fuse_norm=fuse_norm),
        grid=(T // tm, N_GROUP),
        in_specs=[
            pl.BlockSpec((tm, D_MODEL), lambda i, j: (i, 0)),
            pl.BlockSpec((1, D_MODEL), lambda i, j: (0, 0)),
        ] + w_specs + [cast_in, cast_in],
        out_specs=[pl.BlockSpec((spg, tm, LANES), lambda i, j: (j, i, 0))] + list(specs),
        out_shape=[jax.ShapeDtypeStruct((N_SLAB, T, LANES), F32)] + list(shapes),
        compiler_params=pltpu.CompilerParams(
            dimension_semantics=("arbitrary", "arbitrary"),
            vmem_limit_bytes=VMEM_LIMIT),
        name="inproj_norm" if fuse_norm else "inproj",
    )(x, g, *([w_in] * len(w_specs)), w_out, w_pg)


def _conv_kernel(a_ref, cw_ref, o_ref, lo_buf, hi_buf, *, rc):
    seq = a_ref.shape[2]
    n_sub = rc // SUBLANES
    first = HALO - CONV_PAD
    n_off = CONV_WIDTH + (n_sub - 1) * SUBLANES
    edge = rc + HALO
    zero = jnp.zeros((N_SLAB, HALO, LANES), F32)
    lo_buf[:, 0:HALO, :] = zero
    lo_buf[:, HALO:, :] = a_ref[:, 0, 0:edge, :]
    hi_buf[:, 0:edge, :] = a_ref[:, 0, seq - edge:seq, :]
    hi_buf[:, edge:, :] = zero

    for c in range(N_SLAB):
        taps = [jnp.broadcast_to(cw_ref[c, k:k + 1, :], (SUBLANES, LANES))
                for k in range(CONV_WIDTH)]

        def conv_chunk(window, c=c, taps=taps):
            accs = [[None, None] for _ in range(n_sub)]
            for o in [o for ph in range(SUBLANES) for o in range(ph, n_off, SUBLANES)]:
                win = window(o)
                for j in range(n_sub):
                    k = o - j * SUBLANES
                    if 0 <= k < CONV_WIDTH:
                        prod = win * taps[k]
                        cur = accs[j][k % 2]
                        accs[j][k % 2] = prod if cur is None else cur + prod
            return jnp.concatenate([e + o_ for e, o_ in accs], axis=0)

        o_ref[c, 0, 0:rc, :] = conv_chunk(
            lambda o, c=c: lo_buf[c, pl.ds(first + o, SUBLANES, stride=1), :])

        def interior(r, carry, c=c, conv_chunk=conv_chunk):
            r0 = pl.multiple_of(r * rc, rc)
            o_ref[c, 0, pl.ds(r0, rc), :] = conv_chunk(
                lambda o: a_ref[c, 0, pl.ds(r0 - CONV_PAD + o, SUBLANES, stride=1), :])
            return carry

        lax.fori_loop(1, seq // rc - 1, interior, 0)
        o_ref[c, 0, seq - rc:seq, :] = conv_chunk(
            lambda o, c=c: hi_buf[c, pl.ds(first + o, SUBLANES, stride=1), :])


def _conv_branch(a4, cw, *, rc=128):
    _, B, S, _ = a4.shape
    seq_block = pl.BlockSpec((N_SLAB, 1, S, LANES), lambda b: (0, b, 0, 0))
    return pl.pallas_call(
        functools.partial(_conv_kernel, rc=rc),
        grid=(B,),
        in_specs=[seq_block, pl.BlockSpec((N_SLAB, CONV_WIDTH, LANES), lambda b: (0, 0, 0))],
        out_specs=seq_block,
        out_shape=jax.ShapeDtypeStruct((N_SLAB, B, S, LANES), F32),
        scratch_shapes=[pltpu.VMEM((N_SLAB, rc + 2 * HALO, LANES), F32),
                        pltpu.VMEM((N_SLAB, rc + 2 * HALO, LANES), F32)],
        compiler_params=pltpu.CompilerParams(
            dimension_semantics=("arbitrary",),
            vmem_limit_bytes=VMEM_LIMIT),
        name="conv_branch",
    )(a4, cw)


def _attn_kernel(sink_ref, q_ref, kvp_ref, kvc_ref, kvn_ref, gb_ref, bias_ref, o_ref, *, qb):
    step = pl.program_id(1)
    last_blk = pl.num_programs(1) * qb - 1
    nk = 3 * BLK
    lo = lax.broadcasted_iota(jnp.int32, (nk, LANES), 1) < HEAD_DIM
    lo_q = lax.broadcasted_iota(jnp.int32, (2 * BLK, LANES), 1) < HEAD_DIM
    top = lax.broadcasted_iota(jnp.int32, (2 * BLK, 1), 0) < BLK
    zeros = jnp.zeros((nk, LANES), BF16)
    ones_lo = jnp.where(lo, 1.0, 0.0).astype(BF16)
    ones_hi = jnp.where(lo, 0.0, 1.0).astype(BF16)
    sum_cols = jnp.concatenate([ones_lo, ones_hi], axis=0)

    def keys(sub, cols):
        parts = []
        for t in (sub - 1, sub, sub + 1):
            if t < 0:
                parts.append(kvp_ref[0, :, cols])
            elif t >= qb:
                parts.append(kvn_ref[0, :, cols])
            else:
                parts.append(kvc_ref[0, t * BLK:(t + 1) * BLK, cols])
        return jnp.concatenate(parts, axis=0)

    for sub, h in [(sub, h) for sub in range(qb) for h in range(N_KV_HEADS)]:
        blk = step * qb + sub
        variant = jnp.where(blk == 0, 0, jnp.where(blk == last_blk, 2, 1))
        rows = slice(sub * BLK, (sub + 1) * BLK)
        kv = keys(sub, slice(h * HW, h * HW + LANES))
        vk = keys(sub, slice(h * HW + LANES, (h + 1) * HW))
        kblk = jnp.concatenate([jnp.where(lo, kv, zeros), jnp.where(lo, zeros, vk)], axis=0)
        vblk = jnp.concatenate([jnp.where(lo, vk, zeros), jnp.where(lo, zeros, kv)], axis=0)
        vblk = jnp.concatenate([vblk, sum_cols], axis=1)
        hs = slice(h * HW, (h + 1) * HW)
        qh = q_ref[0, rows, hs]
        lhs = jnp.concatenate([qh[:, :LANES], qh[:, LANES:]], axis=0)
        s = lax.dot_general(lhs, kblk, (((1,), (1,)), ((), ())), preferred_element_type=F32)
        s = s + bias_ref[variant, h]
        sa, sb = s[:, :nk], s[:, nk:]
        g0 = h * GQA_GROUP
        sk_a = jnp.where(top, sink_ref[g0], sink_ref[g0 + 2])
        sk_b = jnp.where(top, sink_ref[g0 + 1], sink_ref[g0 + 3])
        ma = jnp.maximum(jnp.max(sa, axis=-1, keepdims=True), sk_a)
        mb = jnp.maximum(jnp.max(sb, axis=-1, keepdims=True), sk_b)
        pe = jnp.concatenate([jnp.exp2(sa - ma), jnp.exp2(sb - mb)], axis=1).astype(BF16)
        o = jnp.dot(pe, vblk, preferred_element_type=F32)
        den = o[:, LANES:] + jnp.where(lo_q, jnp.exp2(sk_a - ma), jnp.exp2(sk_b - mb))
        y = o[:, :LANES] * (1.0 / den)
        yh = jnp.concatenate([y[:BLK], y[BLK:]], axis=1)
        o_ref[0, rows, hs] = (yh * gb_ref[0, rows, hs]).astype(BF16)


def _attn_branch(sink, q3, kv3, gb3, bias4, *, qb=16):
    B, S, _ = q3.shape
    nb = S // BLK
    kvw = kv3.shape[-1]

    def row(width=W_ATT):
        return pl.BlockSpec((1, qb * BLK, width), lambda b, j: (b, j, 0))

    def halo(d):
        return pl.BlockSpec((1, BLK, kvw),
                            lambda b, j: (b, jnp.clip(j * qb + d, 0, nb - 1), 0))

    return pl.pallas_call(
        functools.partial(_attn_kernel, qb=qb),
        grid=(B, nb // qb),
        in_specs=[
            pl.BlockSpec(memory_space=pltpu.SMEM),
            row(), halo(-1), row(kvw), halo(qb), row(),
            pl.BlockSpec(bias4.shape, lambda b, j: (0, 0, 0, 0), pipeline_mode=pl.Buffered(1)),
        ],
        out_specs=row(),
        out_shape=jax.ShapeDtypeStruct((B, S, W_ATT), BF16),
        compiler_params=pltpu.CompilerParams(
            dimension_semantics=("arbitrary", "arbitrary"),
            vmem_limit_bytes=VMEM_LIMIT),
        name="attn_branch",
    )(sink, q3, kv3, kv3, kv3, gb3, bias4)


def _outproj_kernel(h_ref, yc_ref, cb_ref, gz_ref, yb_ref, p_ref, lg_ref, lb_ref, woa_ref,
                    wob_ref, wpe_ref, peg_ref, wpg_ref, ng_ref, *rest, last):
    if last:
        (o_ref,) = rest
    else:
        win32_ref, scale_ref, h_out, hn_out, win_ref = rest
        win_ref[...] = (win32_ref[...] * scale_ref[...]).astype(BF16)
    ys = [yc_ref[c] + cb_ref[c] for c in range(N_SLAB)]
    tot = ys[0]
    for c in range(1, N_SLAB):
        tot = tot + ys[c]
    mu = jnp.sum(tot, axis=-1, keepdims=True) * (1.0 / W_CONV)
    xcs = [y - mu for y in ys]
    sq = xcs[0] * xcs[0]
    for c in range(1, N_SLAB):
        sq = sq + xcs[c] * xcs[c]
    var = jnp.sum(sq, axis=-1, keepdims=True) * (1.0 / W_CONV)
    rs = lax.rsqrt(var + EPS)
    ya = []
    for c in range(N_SLAB):
        cs = slice(c * LANES, (c + 1) * LANES)
        y = (xcs[c] * rs) * lg_ref[:, cs] + lb_ref[:, cs]
        ya.append((_silu(y) * gz_ref[:, cs]).astype(BF16))
    ya = jnp.concatenate(ya, axis=1)

    h1 = h_ref[...] + jnp.dot(yb_ref[...], wob_ref[...], preferred_element_type=F32)
    h1 = h1 + jnp.dot(ya, woa_ref[...], preferred_element_type=F32)
    e = jnp.dot(p_ref[...].astype(BF16), wpe_ref[...], preferred_element_type=F32)
    e = _rmsnorm(e, peg_ref[...])
    gate = jnp.dot(h1.astype(BF16), wpg_ref[...], preferred_element_type=F32)
    h2 = h1 + e * _sigmoid(gate)
    if last:
        o_ref[...] = _rmsnorm(h2, ng_ref[...])
    else:
        h_out[...] = h2
        hn_out[...] = _rmsnorm(h2, ng_ref[...]).astype(BF16)


def _outproj(h, yc, cb, gz, yb, p, lg, lb, wo, wpe, peg, wpg, ng, w_in32, w_scale, layer, *,
             last, tm=512):
    T = h.shape[0]
    steps = T // tm
    wrows = D_MODEL // steps
    assert wrows * steps == D_MODEL and wrows % 16 == 0
    w_in_cols = w_in32.shape[-1]

    def row(width):
        return pl.BlockSpec((tm, width), lambda i: (i, 0))

    def const(shape, blk=0):
        return pl.BlockSpec(shape, lambda i: (blk, 0), pipeline_mode=pl.Buffered(1))

    in_specs = [row(D_MODEL),
                pl.BlockSpec((N_SLAB, tm, LANES), lambda i: (0, i, 0)),
                pl.BlockSpec((N_SLAB, 1, LANES), lambda i: (0, 0, 0),
                             pipeline_mode=pl.Buffered(1)),
                row(W_CONV), row(W_ATT),
                pl.BlockSpec((None, tm, PLE_DIM), lambda i: (layer, i, 0)),
                const((1, W_CONV)), const((1, W_CONV)),
                const((W_CONV, D_MODEL), 0), const((W_ATT, D_MODEL), 1),
                pl.BlockSpec((None, PLE_DIM, D_MODEL), lambda i: (layer, 0, 0),
                             pipeline_mode=pl.Buffered(1)),
                const((1, D_MODEL)), const((D_MODEL, D_MODEL)), const((1, D_MODEL))]
    args = [h, yc, cb, gz, yb, p, lg, lb, wo, wo, wpe, peg, wpg, ng]
    if last:
        out_specs = [row(D_MODEL)]
        out_shape = [jax.ShapeDtypeStruct((T, D_MODEL), F32)]
    else:
        in_specs += [pl.BlockSpec((None, wrows, w_in_cols), lambda i: (layer + 1, i, 0)),
                     const((1, w_in_cols))]
        args += [w_in32, w_scale]
        out_specs = [row(D_MODEL), row(D_MODEL),
                     pl.BlockSpec((wrows, w_in_cols), lambda i: (i, 0))]
        out_shape = [jax.ShapeDtypeStruct((T, D_MODEL), F32),
                     jax.ShapeDtypeStruct((T, D_MODEL), BF16),
                     jax.ShapeDtypeStruct((D_MODEL, w_in_cols), BF16)]
    return pl.pallas_call(
        functools.partial(_outproj_kernel, last=last),
        grid=(steps,),
        in_specs=in_specs,
        out_specs=out_specs,
        out_shape=out_shape,
        compiler_params=pltpu.CompilerParams(
            dimension_semantics=("arbitrary",),
            vmem_limit_bytes=VMEM_LIMIT),
        name="outproj_final" if last else "outproj",
    )(*args)


def _band_buckets():
    q_off = np.arange(BLK)[:, None]
    k_off = np.arange(3 * BLK)[None, :] - BLK
    rel = k_off - q_off
    half = NUM_BUCKETS // 2
    ret = (rel > 0).astype(np.int32) * half
    n = np.abs(rel)
    max_exact = half // 2
    large = max_exact + (np.log(np.maximum(n, 1) / max_exact)
                         / np.log(MAX_DISTANCE / max_exact)
                         * (half - max_exact)).astype(np.int32)
    large = np.minimum(large, half - 1)
    ret = ret + np.where(n < max_exact, n, large)
    return ret.astype(np.int32), (n <= WINDOW)


def _bias_table(rel_bias):
    buckets, band = _band_buckets()
    onehot = np.zeros((BLK * 3 * BLK, NUM_BUCKETS), np.float32)
    onehot[np.arange(onehot.shape[0]), buckets.reshape(-1)] = 1.0
    bias = jnp.dot(rel_bias.astype(F32).T, jnp.asarray(onehot.T),
                   precision=lax.Precision.HIGHEST)
    bias = bias.reshape(N_Q_HEADS, BLK, 3 * BLK)
    bias = jnp.where(jnp.asarray(band)[None], bias * LOG2E, NEG)
    col = np.arange(3 * BLK)
    first = jnp.where(jnp.asarray(col < BLK)[None, None], NEG, bias)
    last = jnp.where(jnp.asarray(col >= 2 * BLK)[None, None], NEG, bias)
    tab = jnp.stack([first, bias, last])
    tab = tab.reshape(3, N_Q_HEADS // 2, 2, BLK, 3 * BLK)
    tab = jnp.concatenate([tab[:, :, 0], tab[:, :, 1]], axis=-1)
    return tab.reshape(3, N_KV_HEADS, 2 * BLK, 6 * BLK)


def _w_in_scale():
    q0 = 3 * W_CONV
    col = np.arange(3 * W_CONV + 2 * W_ATT + 2 * W_KV)
    scale = np.where((col >= q0) & (col < q0 + W_ATT), HEAD_DIM ** -0.5 * LOG2E, 1.0)
    return jnp.asarray(scale, F32).reshape(1, -1)


def kernel(x, p, norm_g, w_in, conv_w, conv_b, cln_g, cln_b, sink, rel_bias,
           w_out, w_pe, pe_g, w_pg, final_g):
    B, S, _ = x.shape
    T = B * S
    bias4 = _bias_table(rel_bias)
    w_scale = _w_in_scale()
    win = (w_in[0] * w_scale).astype(BF16)
    wpe = w_pe.astype(BF16)
    cw = jnp.transpose(conv_w.reshape(DEPTH, CONV_WIDTH, N_SLAB, LANES), (0, 2, 1, 3))
    p3 = p.reshape(DEPTH, T, PLE_DIM)
    h = x.reshape(T, D_MODEL)
    hn = h
    for i in range(DEPTH):
        a, gz, gb, q, kv, wo, wpg = _inproj(hn, norm_g[i].reshape(1, D_MODEL), win, w_out, w_pg,
                                            i, fuse_norm=(i == 0))
        yc = _conv_branch(a.reshape(N_SLAB, B, S, LANES), cw[i])
        yb = _attn_branch(sink[i] * LOG2E, q.reshape(B, S, W_ATT),
                          kv.reshape(B, S, N_KV_HEADS * HW), gb.reshape(B, S, W_ATT), bias4)
        last = i == DEPTH - 1
        ng = final_g if last else norm_g[i + 1]
        outs = _outproj(h, yc.reshape(N_SLAB, T, LANES), conv_b[i].reshape(N_SLAB, 1, LANES),
                        gz, yb.reshape(T, W_ATT), p3,
                        cln_g[i].reshape(1, W_CONV), cln_b[i].reshape(1, W_CONV),
                        wo, wpe, pe_g[i].reshape(1, D_MODEL), wpg, ng.reshape(1, D_MODEL),
                        w_in, w_scale, i, last=last)
        if last:
            (h,) = outs
        else:
            h, hn, win = outs
    return h.reshape(B, S, D_MODEL)
```
